```python
import jax, jax.numpy as jnp
from jax import lax
import numpy as np

D_MODEL = 1024
BATCH = 8
SEQ = 2048
DEPTH = 1

GRID_W = 64
HEAD_DIM = 64
D_MIX = D_MODEL
A_HEADS = 8
A_KV_HEADS = 2
B_HEADS = 8
A_WIDTH = A_HEADS * HEAD_DIM
A_KV_WIDTH = A_KV_HEADS * HEAD_DIM
B_WIDTH = B_HEADS * HEAD_DIM
SPLITS = (A_WIDTH, A_KV_WIDTH, A_KV_WIDTH, A_WIDTH, B_WIDTH, B_WIDTH, B_WIDTH, B_WIDTH)
D_IN = sum(SPLITS)
Q_BLOCK = 128
AXIS_DIM = HEAD_DIM // 2
ROPE_PAIRS = AXIS_DIM // 2
ROPE_THETA = 10000.0
NA_KH = 8
NA_KW = 16
NA_COL_BLOCK = 16
NA_BAND = NA_COL_BLOCK + NA_KW
EPS = 1e-6

kernel_name = "hymba_gqa_axialrope_natten_encoder"


def _rmsnorm(x, gain):
    x32 = x.astype(jnp.float32)
    y = x32 * lax.rsqrt(jnp.mean(x32 * x32, axis=-1, keepdims=True) + EPS)
    return (y * gain.astype(jnp.float32)).astype(x.dtype)


def _axial_rope_tables(seq, dtype):
    t = jnp.arange(seq, dtype=jnp.int32)
    row = (t // GRID_W).astype(jnp.float32)
    col = (t % GRID_W).astype(jnp.float32)
    inv = ROPE_THETA ** (-jnp.arange(ROPE_PAIRS, dtype=jnp.float32) * (2.0 / AXIS_DIM))
    ang_r = row[:, None] * inv[None, :]
    ang_c = col[:, None] * inv[None, :]
    ang = jnp.concatenate([ang_r, ang_r, ang_c, ang_c], axis=-1)
    return jnp.cos(ang)[:, None, :].astype(dtype), jnp.sin(ang)[:, None, :].astype(dtype)


def _rotate_half_axial(x):
    shp = x.shape
    xr = x.reshape(shp[:-1] + (2, 2, ROPE_PAIRS))
    x1 = xr[..., 0, :]
    x2 = xr[..., 1, :]
    return jnp.stack([-x2, x1], axis=-2).reshape(shp)


def _gqa_blocked(q, k, v):
    b, s, h, d = q.shape
    hk = k.shape[2]
    g = h // hk
    nb = s // Q_BLOCK
    qb = q.reshape(b, nb, Q_BLOCK, hk, g, d).transpose(1, 0, 2, 3, 4, 5)
    scale = d ** -0.5

    def block(qi):
        sc = jnp.einsum('bqkgd,bskd->bkgqs', qi, k).astype(jnp.float32) * scale
        p = jax.nn.softmax(sc, axis=-1).astype(v.dtype)
        return jnp.einsum('bkgqs,bskd->bqkgd', p, v)

    o = lax.map(block, qb)
    return o.transpose(1, 0, 2, 3, 4, 5).reshape(b, s, h * d)


def _neighbourhood_attn(q, k, v, rpb):
    b, s, h, d = q.shape
    rows = s // GRID_W
    kh = min(NA_KH, rows)
    n_cb = GRID_W // NA_COL_BLOCK
    qg = q.reshape(b, rows, GRID_W, h, d)
    kg = k.reshape(b, rows, GRID_W, h, d)
    vg = v.reshape(b, rows, GRID_W, h, d)
    c = np.arange(GRID_W)
    col_start = np.clip(c - NA_KW // 2, 0, GRID_W - NA_KW)
    band_start = np.clip(np.arange(n_cb) * NA_COL_BLOCK - NA_KW // 2, 0, GRID_W - NA_BAND)
    band_cols = band_start[:, None] + np.arange(NA_BAND)[None, :]
    qc = c.reshape(n_cb, NA_COL_BLOCK)
    qcs = col_start.reshape(n_cb, NA_COL_BLOCK)
    kc = band_cols[:, None, :]
    col_valid = (kc >= qcs[..., None]) & (kc < qcs[..., None] + NA_KW)
    col_idx = np.clip(kc - qc[..., None] + (NA_KW - 1), 0, 2 * NA_KW - 2)
    rpb_col = rpb[:, :, col_idx]
    mask = jnp.asarray(col_valid[:, :, None, :])
    scale = d ** -0.5

    def row_block(args):
        r, q_row = args
        r0 = jnp.clip(r - kh // 2, 0, rows - kh)
        k_rows = lax.dynamic_slice_in_dim(kg, r0, kh, axis=1)
        v_rows = lax.dynamic_slice_in_dim(vg, r0, kh, axis=1)
        k_band = jnp.take(k_rows, band_cols, axis=2)
        v_band = jnp.take(v_rows, band_cols, axis=2)
        qb = q_row.reshape(b, n_cb, NA_COL_BLOCK, h, d)
        sc = jnp.einsum('bjqhd,bkjchd->bhjqkc', qb, k_band).astype(jnp.float32) * scale
        row_idx = r0 + jnp.arange(kh) - r + (NA_KH - 1)
        bias = rpb_col[:, row_idx].transpose(0, 2, 3, 1, 4)
        sc = jnp.where(mask, sc + bias.astype(jnp.float32), -jnp.inf)
        p = jax.nn.softmax(sc, axis=(-2, -1)).astype(v.dtype)
        o = jnp.einsum('bhjqkc,bkjchd->bjqhd', p, v_band)
        return o.reshape(b, GRID_W, h * d)

    o = lax.map(row_block, (jnp.arange(rows), qg.transpose(1, 0, 2, 3, 4)))
    return o.transpose(1, 0, 2, 3).reshape(b, s, h * d)


def setup_inputs(seed: int = 0) -> dict:
    key = jax.random.key(seed)
    ks = jax.random.split(key, 9)
    x = jax.random.normal(ks[0], (BATCH, SEQ, D_MODEL), jnp.float32)
    norm_gain = 1.0 + 0.02 * jax.random.normal(ks[1], (DEPTH, D_MODEL), jnp.float32)
    w_in = jax.random.normal(ks[2], (DEPTH, D_MODEL, D_IN), jnp.float32) * D_MODEL ** -0.5
    q_norm_a = 1.0 + 0.02 * jax.random.normal(ks[3], (DEPTH, HEAD_DIM), jnp.float32)
    k_norm_a = 1.0 + 0.02 * jax.random.normal(ks[4], (DEPTH, HEAD_DIM), jnp.float32)
    na_rpb = 0.02 * jax.random.normal(ks[5], (DEPTH, B_HEADS, 2 * NA_KH - 1, 2 * NA_KW - 1), jnp.float32)
    w_out = jax.random.normal(ks[6], (DEPTH, D_MIX, D_MODEL), jnp.float32) * D_MIX ** -0.5
    final_norm_gain = 1.0 + 0.02 * jax.random.normal(ks[7], (D_MODEL,), jnp.float32)
    return {"x": x, "norm_gain": norm_gain, "w_in": w_in, "q_norm_a": q_norm_a,
            "k_norm_a": k_norm_a, "na_rpb": na_rpb, "w_out": w_out,
            "final_norm_gain": final_norm_gain}


def reference(x, norm_gain, w_in, q_norm_a, k_norm_a, na_rpb, w_out, final_norm_gain):
    b, s, _ = x.shape
    cos, sin = _axial_rope_tables(s, x.dtype)
    split_at = [int(v) for v in np.cumsum(SPLITS)[:-1]]
    for l in range(DEPTH):
        h = _rmsnorm(x, norm_gain[l])
        proj = h @ w_in[l]
        q_a, k_a, v_a, g_a, q_b, k_b, v_b, g_b = jnp.split(proj, split_at, axis=-1)
        q_a = _rmsnorm(q_a.reshape(b, s, A_HEADS, HEAD_DIM), q_norm_a[l])
        k_a = _rmsnorm(k_a.reshape(b, s, A_KV_HEADS, HEAD_DIM), k_norm_a[l])
        q_a = q_a * cos + _rotate_half_axial(q_a) * sin
        k_a = k_a * cos + _rotate_half_axial(k_a) * sin
        v_a = v_a.reshape(b, s, A_KV_HEADS, HEAD_DIM)
        o_a = _gqa_blocked(q_a, k_a, v_a) * jax.nn.silu(g_a)
        o_b = _neighbourhood_attn(q_b.reshape(b, s, B_HEADS, HEAD_DIM),
                                  k_b.reshape(b, s, B_HEADS, HEAD_DIM),
                                  v_b.reshape(b, s, B_HEADS, HEAD_DIM), na_rpb[l])
        o_b = o_b * jax.nn.silu(g_b)
        mixed = jnp.concatenate([o_a, o_b], axis=-1)
        x = x + mixed @ w_out[l]
    return _rmsnorm(x, final_norm_gain)
```

```python
import functools

import jax
import jax.numpy as jnp
import numpy as np
from jax import lax
from jax.experimental import pallas as pl
from jax.experimental.pallas import tpu as pltpu

D_MODEL = 1024
SEQ = 2048
GRID_W = 64
GRID_ROWS = SEQ // GRID_W
HEAD_DIM = 64
A_HEADS = 8
A_KV_HEADS = 2
B_HEADS = 8
A_WIDTH = A_HEADS * HEAD_DIM
A_KV_WIDTH = A_KV_HEADS * HEAD_DIM
B_WIDTH = B_HEADS * HEAD_DIM
ROPE_PAIRS = HEAD_DIM // 4
ROPE_THETA = 10000.0
NA_KH = 8
NA_KW = 16
EPS = 1e-6
SCALE = HEAD_DIM ** -0.5

LANES = 128
A_HEAD_ORDER = (0, 4, 1, 5, 2, 6, 3, 7)
MASK_VALUE = -1e30

IN_TM = 512
GQA_TQ = 256
NA_ROWS = 8
NA_TQ = NA_ROWS * GRID_W
NA_WIN_ROWS = 16
NA_TK = NA_WIN_ROWS * GRID_W
NA_STRIP_BLOCKS = 32
OUT_TM = 512

C_QA = 0
C_KVA = C_QA + A_WIDTH
C_G = C_KVA + 2 * A_KV_WIDTH
C_B = C_G + A_WIDTH + B_WIDTH
C_END = C_B + 3 * B_WIDTH


def _lane_lo(shape):
    return lax.broadcasted_iota(jnp.int32, shape, len(shape) - 1) % LANES < HEAD_DIM


def _silu(g):
    return g * (1.0 / (1.0 + jnp.exp(-g)))


def _head_norm_rope(y, gain, cos, sin_signed):
    lo = _lane_lo((1, LANES))
    y2 = y * y
    s_lo = jnp.sum(jnp.where(lo, y2, 0.0), axis=-1, keepdims=True)
    s_hi = jnp.sum(jnp.where(lo, 0.0, y2), axis=-1, keepdims=True)
    ms = jnp.where(lo, s_lo, s_hi) * (1.0 / HEAD_DIM)
    yn = (y * lax.rsqrt(ms + EPS)) * gain
    lane = lax.broadcasted_iota(jnp.int32, (1, LANES), 1)
    first_half = lane % (2 * ROPE_PAIRS) < ROPE_PAIRS
    nxt = pltpu.roll(yn, LANES - ROPE_PAIRS, 1)
    prv = pltpu.roll(yn, ROPE_PAIRS, 1)
    rot = jnp.where(first_half, nxt, prv)
    return yn * cos + rot * sin_signed


def _in_proj_kernel(x_ref, gain_ref, w_ref, qgain_ref, kgain_ref, cos_ref, sin_ref,
                    qa_ref, kva_ref, g_ref, qkvb_ref):
    x = x_ref[0]
    ms = jnp.mean(x * x, axis=-1, keepdims=True)
    h = ((x * lax.rsqrt(ms + EPS)) * gain_ref[...]).astype(jnp.bfloat16)
    cos = cos_ref[...]
    sin = sin_ref[...]

    qa = jnp.dot(h, w_ref[:, C_QA:C_KVA], preferred_element_type=jnp.float32)
    for j in range(A_WIDTH // LANES):
        sl = slice(j * LANES, (j + 1) * LANES)
        qa_ref[0, :, sl] = _head_norm_rope(qa[:, sl], qgain_ref[...], cos, sin).astype(jnp.bfloat16)

    kva = jnp.dot(h, w_ref[:, C_KVA:C_G], preferred_element_type=jnp.float32)
    kva_ref[0, :, :LANES] = _head_norm_rope(kva[:, :LANES], kgain_ref[...], cos, sin).astype(jnp.bfloat16)
    kva_ref[0, :, LANES:] = kva[:, LANES:].astype(jnp.bfloat16)

    g_ref[0] = jnp.dot(h, w_ref[:, C_G:C_B], preferred_element_type=jnp.float32)
    qkvb_ref[0] = jnp.dot(h, w_ref[:, C_B:C_END], preferred_element_type=jnp.float32).astype(jnp.bfloat16)


def _in_proj(x, gain, w_all, qgain, kgain, cos, sin_signed):
    b, s, d = x.shape
    n_s = s // IN_TM
    row = lambda bi, si: (bi, si, 0)
    const2 = lambda bi, si: (0, 0)
    return pl.pallas_call(
        _in_proj_kernel,
        grid=(b, n_s),
        in_specs=[
            pl.BlockSpec((1, IN_TM, d), row),
            pl.BlockSpec((1, d), const2),
            pl.BlockSpec((d, C_END), const2),
            pl.BlockSpec((1, LANES), const2),
            pl.BlockSpec((1, LANES), const2),
            pl.BlockSpec((IN_TM, LANES), lambda bi, si: (si, 0)),
            pl.BlockSpec((IN_TM, LANES), lambda bi, si: (si, 0)),
        ],
        out_specs=[
            pl.BlockSpec((1, IN_TM, A_WIDTH), row),
            pl.BlockSpec((1, IN_TM, 2 * A_KV_WIDTH), row),
            pl.BlockSpec((1, IN_TM, A_WIDTH + B_WIDTH), row),
            pl.BlockSpec((1, IN_TM, 3 * B_WIDTH), row),
        ],
        out_shape=[
            jax.ShapeDtypeStruct((b, s, A_WIDTH), jnp.bfloat16),
            jax.ShapeDtypeStruct((b, s, 2 * A_KV_WIDTH), jnp.bfloat16),
            jax.ShapeDtypeStruct((b, s, A_WIDTH + B_WIDTH), jnp.float32),
            jax.ShapeDtypeStruct((b, s, 3 * B_WIDTH), jnp.bfloat16),
        ],
        compiler_params=pltpu.CompilerParams(
            dimension_semantics=("arbitrary", "arbitrary"),
            vmem_limit_bytes=48 * 1024 * 1024),
        name="in_proj",
    )(x, gain, w_all, qgain, kgain, cos, sin_signed)


def _pair_attention(q, k, v, bias):
    lo = _lane_lo((1, LANES))
    one = jnp.ones((), v.dtype)
    zero = jnp.zeros((), q.dtype)
    halves = []
    for is_lo in (True, False):
        qm = jnp.where(lo, q, zero) if is_lo else jnp.where(lo, zero, q)
        vm = jnp.where(lo, v, one) if is_lo else jnp.where(lo, one, v)
        s = lax.dot_general(qm, k, (((1,), (1,)), ((), ())), preferred_element_type=jnp.float32)
        if bias is not None:
            s = s + bias[0 if is_lo else 1]
        m = jnp.max(s, axis=-1, keepdims=True)
        p = jnp.exp(s - m).astype(jnp.bfloat16)
        halves.append(jnp.dot(p, vm, preferred_element_type=jnp.float32))
    a, b = halves
    numer = jnp.where(lo, a, b)
    denom = jnp.where(lo, pltpu.roll(a, HEAD_DIM, 1), pltpu.roll(b, HEAD_DIM, 1))
    return numer / denom


def _gqa_kernel(qa_ref, kva_ref, g_ref, o_ref):
    k = kva_ref[0, :, :LANES]
    v = kva_ref[0, :, LANES:]
    for j in range(A_WIDTH // LANES):
        sl = slice(j * LANES, (j + 1) * LANES)
        o = _pair_attention(qa_ref[0, :, sl], k, v, None)
        o_ref[0, :, sl] = (o * _silu(g_ref[0, :, sl])).astype(jnp.bfloat16)


def _gqa(qa, kva, g):
    b, s, _ = qa.shape
    return pl.pallas_call(
        _gqa_kernel,
        grid=(b, s // GQA_TQ),
        in_specs=[
            pl.BlockSpec((1, GQA_TQ, A_WIDTH), lambda bi, qi: (bi, qi, 0)),
            pl.BlockSpec((1, s, 2 * A_KV_WIDTH), lambda bi, qi: (bi, 0, 0)),
            pl.BlockSpec((1, GQA_TQ, A_WIDTH), lambda bi, qi: (bi, qi, 0)),
        ],
        out_specs=pl.BlockSpec((1, GQA_TQ, A_WIDTH), lambda bi, qi: (bi, qi, 0)),
        out_shape=jax.ShapeDtypeStruct((b, s, A_WIDTH), jnp.bfloat16),
        compiler_params=pltpu.CompilerParams(
            dimension_semantics=("arbitrary", "arbitrary"),
            vmem_limit_bytes=48 * 1024 * 1024),
        name="gqa",
    )(qa, kva, g)


def _na_window_start(i):
    return jnp.clip(i * NA_ROWS - NA_KH // 2, 0, GRID_ROWS - NA_WIN_ROWS)


def _na_build_bias(strip_ref, bias_ref, block):
    win0 = int(np.clip(block * NA_ROWS - NA_KH // 2, 0, GRID_ROWS - NA_WIN_ROWS))
    kl = lax.broadcasted_iota(jnp.int32, (1, NA_TK), 1) // GRID_W
    for a in range(NA_ROWS):
        qr = block * NA_ROWS + a
        r0 = int(np.clip(qr - NA_KH // 2, 0, GRID_ROWS - NA_KH))
        valid = (kl >= r0 - win0) & (kl < r0 - win0 + NA_KH)
        t0 = win0 - qr + (NA_STRIP_BLOCKS // 2 - 1)
        par = t0 % 2
        c0 = (t0 - par) * GRID_W
        for hh in range(2):
            tile = strip_ref[hh, par, :, c0:c0 + NA_TK]
            bias_ref[hh, a * GRID_W:(a + 1) * GRID_W, :] = jnp.where(valid, tile, MASK_VALUE)


def _natten_kernel(q_ref, k_ref, v_ref, g_ref, strip_ref, o_ref, bias_ref):
    i = pl.program_id(0)
    bi = pl.program_id(2)
    n_blocks = GRID_ROWS // NA_ROWS

    @pl.when(bi == 0)
    def _():
        @pl.when(i == 0)
        def _():
            _na_build_bias(strip_ref, bias_ref, 0)

        @pl.when(i == n_blocks - 1)
        def _():
            _na_build_bias(strip_ref, bias_ref, n_blocks - 1)

        @pl.when((i > 0) & (i < n_blocks - 1))
        def _():
            _na_build_bias(strip_ref, bias_ref, 1)

    start = pl.multiple_of(_na_window_start(i) * GRID_W, 4 * GRID_W)
    k = k_ref[0, pl.ds(start, NA_TK), :]
    v = v_ref[0, pl.ds(start, NA_TK), :]
    o = _pair_attention(q_ref[0], k, v, (bias_ref[0], bias_ref[1]))
    o_ref[0] = (o * _silu(g_ref[0])).astype(jnp.bfloat16)


def _natten(qkvb, g, strips):
    b, s, _ = qkvb.shape
    n_pairs = B_WIDTH // LANES
    g_col0 = A_WIDTH // LANES
    return pl.pallas_call(
        _natten_kernel,
        grid=(s // NA_TQ, n_pairs, b),
        in_specs=[
            pl.BlockSpec((1, NA_TQ, LANES), lambda i, p, bi: (bi, i, p)),
            pl.BlockSpec((1, s, LANES), lambda i, p, bi: (bi, 0, n_pairs + p)),
            pl.BlockSpec((1, s, LANES), lambda i, p, bi: (bi, 0, 2 * n_pairs + p)),
            pl.BlockSpec((1, NA_TQ, LANES), lambda i, p, bi: (bi, i, g_col0 + p)),
            pl.BlockSpec((2, 2, GRID_W, NA_STRIP_BLOCKS * GRID_W), lambda i, p, bi: (p, 0, 0, 0)),
        ],
        out_specs=pl.BlockSpec((1, NA_TQ, LANES), lambda i, p, bi: (bi, i, p)),
        out_shape=jax.ShapeDtypeStruct((b, s, B_WIDTH), jnp.bfloat16),
        scratch_shapes=[pltpu.VMEM((2, NA_TQ, NA_TK), jnp.float32)],
        compiler_params=pltpu.CompilerParams(
            dimension_semantics=("arbitrary", "arbitrary", "arbitrary"),
            vmem_limit_bytes=48 * 1024 * 1024),
        name="natten",
    )(qkvb, qkvb, qkvb, g, strips)


def _na_bias_strips(rpb):
    c = np.arange(GRID_W)
    col_start = np.clip(c - NA_KW // 2, 0, GRID_W - NA_KW)
    kc = c[None, :]
    col_valid = (kc >= col_start[:, None]) & (kc < col_start[:, None] + NA_KW)
    col_idx = np.clip(kc - c[:, None] + (NA_KW - 1), 0, 2 * NA_KW - 2)
    bc = jnp.where(col_valid[None, None], rpb[:, :, col_idx], MASK_VALUE)
    n_dr = 2 * NA_KH - 1
    flat = bc.transpose(0, 2, 1, 3).reshape(B_HEADS, GRID_W, n_dr * GRID_W)
    left = (NA_STRIP_BLOCKS // 2 - 1) - (NA_KH - 1)
    right = NA_STRIP_BLOCKS - left - n_dr
    strip = jnp.pad(flat, ((0, 0), (0, 0), (left * GRID_W, right * GRID_W)))
    shifted = jnp.pad(strip[:, :, GRID_W:], ((0, 0), (0, 0), (0, GRID_W)))
    return jnp.stack([strip, shifted], axis=1)


def _out_proj_kernel(x_ref, ma_ref, mb_ref, wa_ref, wb_ref, gain_ref, o_ref):
    y = x_ref[0]
    y = y + jnp.dot(ma_ref[0], wa_ref[...], preferred_element_type=jnp.float32)
    y = y + jnp.dot(mb_ref[0], wb_ref[...], preferred_element_type=jnp.float32)
    ms = jnp.mean(y * y, axis=-1, keepdims=True)
    o_ref[0] = (y * lax.rsqrt(ms + EPS)) * gain_ref[...]


def _out_proj(x, ma, mb, wa, wb, gain):
    b, s, d = x.shape
    row = lambda bi, si: (bi, si, 0)
    const2 = lambda bi, si: (0, 0)
    return pl.pallas_call(
        _out_proj_kernel,
        grid=(b, s // OUT_TM),
        in_specs=[
            pl.BlockSpec((1, OUT_TM, d), row),
            pl.BlockSpec((1, OUT_TM, A_WIDTH), row),
            pl.BlockSpec((1, OUT_TM, B_WIDTH), row),
            pl.BlockSpec((A_WIDTH, d), const2),
            pl.BlockSpec((B_WIDTH, d), const2),
            pl.BlockSpec((1, d), const2),
        ],
        out_specs=pl.BlockSpec((1, OUT_TM, d), row),
        out_shape=jax.ShapeDtypeStruct((b, s, d), jnp.float32),
        compiler_params=pltpu.CompilerParams(
            dimension_semantics=("arbitrary", "arbitrary"),
            vmem_limit_bytes=48 * 1024 * 1024),
        name="out_proj",
    )(x, ma, mb, wa, wb, gain)


def _rope_tables(seq):
    t = np.arange(seq)
    inv = ROPE_THETA ** (-jnp.arange(ROPE_PAIRS, dtype=jnp.float32) * (2.0 / (HEAD_DIM // 2)))
    ang_r = jnp.asarray(t // GRID_W, jnp.float32)[:, None] * inv[None, :]
    ang_c = jnp.asarray(t % GRID_W, jnp.float32)[:, None] * inv[None, :]
    ang = jnp.concatenate([ang_r, ang_r, ang_c, ang_c], axis=-1)
    sign = np.where(np.arange(HEAD_DIM) % (2 * ROPE_PAIRS) < ROPE_PAIRS, -1.0, 1.0).astype(np.float32)
    cos = jnp.tile(jnp.cos(ang), (1, LANES // HEAD_DIM))
    sin_signed = jnp.tile(jnp.sin(ang) * sign, (1, LANES // HEAD_DIM))
    return cos, sin_signed


def kernel(x, norm_gain, w_in, q_norm_a, k_norm_a, na_rpb, w_out, final_norm_gain):
    assert w_in.shape[0] == 1, "single-layer trunk only"
    seq = x.shape[1]
    l = 0
    head_cols = np.concatenate([np.arange(HEAD_DIM) + HEAD_DIM * h for h in A_HEAD_ORDER])
    cos, sin_signed = _rope_tables(seq)
    o_qa, o_ka, o_va, o_ga = 0, A_WIDTH, A_WIDTH + A_KV_WIDTH, A_WIDTH + 2 * A_KV_WIDTH
    o_qb = o_ga + A_WIDTH
    o_kb, o_vb, o_gb = o_qb + B_WIDTH, o_qb + 2 * B_WIDTH, o_qb + 3 * B_WIDTH
    w = w_in[l]
    w_all = jnp.concatenate([
        w[:, o_qa:o_ka][:, head_cols],
        w[:, o_ka:o_ga],
        w[:, o_ga:o_qb][:, head_cols],
        w[:, o_gb:o_gb + B_WIDTH],
        w[:, o_qb:o_kb] * SCALE,
        w[:, o_kb:o_gb],
    ], axis=1).astype(jnp.bfloat16)
    qgain = jnp.tile(q_norm_a[l] * SCALE, LANES // HEAD_DIM)[None, :]
    kgain = jnp.tile(k_norm_a[l], LANES // HEAD_DIM)[None, :]
    qa, kva, g, qkvb = _in_proj(x, norm_gain[l][None, :], w_all, qgain, kgain, cos, sin_signed)
    mixed_a = _gqa(qa, kva, g)
    mixed_b = _natten(qkvb, g, _na_bias_strips(na_rpb[l]))
    wo = w_out[l]
    wo_a = wo[:A_WIDTH][head_cols].astype(jnp.bfloat16)
    wo_b = wo[A_WIDTH:].astype(jnp.bfloat16)
    return _out_proj(x, mixed_a, mixed_b, wo_a, wo_b, final_norm_gain[None, :])
```

```python
import math

import jax
import jax.numpy as jnp
import numpy as np
from jax import lax
from jax.experimental import pallas as pl
from jax.experimental.pallas import tpu as pltpu

D_MODEL = 1024
SEQ = 2048
GRID_W = 64
GRID_ROWS = SEQ // GRID_W
HEAD_DIM = 64
A_HEADS = 8
A_KV_HEADS = 2
B_HEADS = 8
A_WIDTH = A_HEADS * HEAD_DIM
A_KV_WIDTH = A_KV_HEADS * HEAD_DIM
B_WIDTH = B_HEADS * HEAD_DIM
ROPE_PAIRS = HEAD_DIM // 4
ROPE_THETA = 10000.0
NA_KH = 8
NA_KW = 16
EPS = 1e-6
LOG2E = math.log2(math.e)
Q_SCALE = HEAD_DIM ** -0.5 * LOG2E

LANES = 128
MASK_VALUE = -1e30

IN_TM = 512
GQA_TQ = 256
NA_ROWS = 8
NA_TQ = NA_ROWS * GRID_W
NA_SUB_ROWS = 4
NA_SUB_WIN = NA_SUB_ROWS + NA_KH
NA_SUB_TQ = NA_SUB_ROWS * GRID_W
NA_SUB_TK = NA_SUB_WIN * GRID_W
NA_STRIP_BLOCKS = 32
OUT_TM = 512
VMEM_LIMIT = 48 * 1024 * 1024

C_QA = 0
C_KA = C_QA + A_WIDTH
C_GA = C_KA + 2 * A_KV_WIDTH
C_QB = C_GA + A_WIDTH
C_KB = C_QB + B_WIDTH
C_GB = C_KB + 2 * B_WIDTH
C_END = C_GB + B_WIDTH


def _lane_lo():
    return lax.broadcasted_iota(jnp.int32, (1, LANES), 1) < HEAD_DIM


def _silu(g):
    return g * (1.0 / (1.0 + jnp.exp(-g)))


def _head_norm_rope(y, gain, cos, sin_signed):
    lo = _lane_lo()
    y2 = y * y
    s_lo = jnp.sum(jnp.where(lo, y2, 0.0), axis=-1, keepdims=True)
    s_hi = jnp.sum(jnp.where(lo, 0.0, y2), axis=-1, keepdims=True)
    ms = jnp.where(lo, s_lo, s_hi) * (1.0 / HEAD_DIM)
    yn = (y * lax.rsqrt(ms + EPS)) * gain
    lane = lax.broadcasted_iota(jnp.int32, (1, LANES), 1)
    first_half = lane % (2 * ROPE_PAIRS) < ROPE_PAIRS
    nxt = pltpu.roll(yn, LANES - ROPE_PAIRS, 1)
    prv = pltpu.roll(yn, ROPE_PAIRS, 1)
    rot = jnp.where(first_half, nxt, prv)
    return yn * cos + rot * sin_signed


def _dup_heads(y):
    lo = _lane_lo()
    swapped = pltpu.roll(y, HEAD_DIM, 1)
    return jnp.where(lo, y, swapped), jnp.where(lo, swapped, y)


def _in_proj_kernel(x_ref, gain_ref, w_ref, qgain_ref, kgain_ref, cos_ref, sin_ref,
                    qa_ref, kva_ref, ga_ref, gb_ref, qkvb_ref):
    x = x_ref[0]
    ms = jnp.mean(x * x, axis=-1, keepdims=True)
    h = ((x * lax.rsqrt(ms + EPS)) * gain_ref[...]).astype(jnp.bfloat16)
    cos = cos_ref[...]
    sin = sin_ref[...]
    bf16 = jnp.bfloat16

    qkva = jnp.dot(h, w_ref[:, C_QA:C_GA], preferred_element_type=jnp.float32)
    for j in range(A_WIDTH // LANES):
        sl = slice(j * LANES, (j + 1) * LANES)
        qa_ref[0, :, sl] = _head_norm_rope(qkva[:, sl], qgain_ref[...], cos, sin).astype(bf16)
    ka = _head_norm_rope(qkva[:, C_KA:C_KA + LANES], kgain_ref[...], cos, sin)
    va = qkva[:, C_KA + LANES:C_GA]
    for n, slab in enumerate(_dup_heads(ka) + _dup_heads(va)):
        kva_ref[0, :, n * LANES:(n + 1) * LANES] = slab.astype(bf16)

    ga_ref[0] = jnp.dot(h, w_ref[:, C_GA:C_QB], preferred_element_type=jnp.float32)
    qb = jnp.dot(h, w_ref[:, C_QB:C_KB], preferred_element_type=jnp.float32)
    qkvb_ref[0, :, :B_WIDTH] = (qb * Q_SCALE).astype(bf16)
    qkvb_ref[0, :, B_WIDTH:] = jnp.dot(h, w_ref[:, C_KB:C_GB], preferred_element_type=jnp.float32).astype(bf16)
    gb_ref[0] = jnp.dot(h, w_ref[:, C_GB:C_END], preferred_element_type=jnp.float32)


def _in_proj(x, gain, w, qgain, kgain, cos, sin_signed):
    b, s, d = x.shape
    row = lambda bi, si: (bi, si, 0)
    const2 = lambda bi, si: (0, 0)
    tab = lambda bi, si: (si, 0)
    widths = (A_WIDTH, 4 * LANES, A_WIDTH, B_WIDTH, 3 * B_WIDTH)
    dtypes = (jnp.bfloat16, jnp.bfloat16, jnp.float32, jnp.float32, jnp.bfloat16)
    return pl.pallas_call(
        _in_proj_kernel,
        grid=(b, s // IN_TM),
        in_specs=[
            pl.BlockSpec((1, IN_TM, d), row),
            pl.BlockSpec((1, d), const2),
            pl.BlockSpec((d, C_END), const2),
            pl.BlockSpec((1, LANES), const2),
            pl.BlockSpec((1, LANES), const2),
            pl.BlockSpec((IN_TM, LANES), tab),
            pl.BlockSpec((IN_TM, LANES), tab),
        ],
        out_specs=[pl.BlockSpec((1, IN_TM, n), row) for n in widths],
        out_shape=[jax.ShapeDtypeStruct((b, s, n), t) for n, t in zip(widths, dtypes)],
        compiler_params=pltpu.CompilerParams(
            dimension_semantics=("arbitrary", "arbitrary"), vmem_limit_bytes=VMEM_LIMIT),
        name="in_proj",
    )(x, gain, w, qgain, kgain, cos, sin_signed)


def _pair_attention(q, k, v, bias):
    lo = _lane_lo()
    one = jnp.ones((), v.dtype)
    zero = jnp.zeros((), q.dtype)
    halves = []
    for is_lo in (True, False):
        qm = jnp.where(lo, q, zero) if is_lo else jnp.where(lo, zero, q)
        vm = jnp.where(lo, v, one) if is_lo else jnp.where(lo, one, v)
        s = lax.dot_general(qm, k, (((1,), (1,)), ((), ())), preferred_element_type=jnp.float32)
        if bias is not None:
            s = s + bias[0 if is_lo else 1]
        m = jnp.max(s, axis=-1, keepdims=True)
        p = jnp.exp2(s - m).astype(jnp.bfloat16)
        halves.append(jnp.dot(p, vm, preferred_element_type=jnp.float32))
    a, b = halves
    numer = jnp.where(lo, a, b)
    denom = jnp.where(lo, pltpu.roll(a, HEAD_DIM, 1), pltpu.roll(b, HEAD_DIM, 1))
    return numer / denom


def _gqa_kernel(qa_ref, kva_ref, g_ref, o_ref):
    pairs_per_kv = (A_HEADS // A_KV_HEADS) // 2
    for j in range(A_WIDTH // LANES):
        kv = j // pairs_per_kv
        k = kva_ref[0, :, kv * LANES:(kv + 1) * LANES]
        v = kva_ref[0, :, (A_KV_HEADS + kv) * LANES:(A_KV_HEADS + kv + 1) * LANES]
        sl = slice(j * LANES, (j + 1) * LANES)
        o = _pair_attention(qa_ref[0, :, sl], k, v, None)
        o_ref[0, :, sl] = (o * _silu(g_ref[0, :, sl])).astype(jnp.bfloat16)


def _gqa(qa, kva, ga):
    b, s, _ = qa.shape
    tile = lambda bi, qi: (bi, qi, 0)
    return pl.pallas_call(
        _gqa_kernel,
        grid=(b, s // GQA_TQ),
        in_specs=[
            pl.BlockSpec((1, GQA_TQ, A_WIDTH), tile),
            pl.BlockSpec((1, s, kva.shape[2]), lambda bi, qi: (bi, 0, 0)),
            pl.BlockSpec((1, GQA_TQ, A_WIDTH), tile),
        ],
        out_specs=pl.BlockSpec((1, GQA_TQ, A_WIDTH), tile),
        out_shape=jax.ShapeDtypeStruct((b, s, A_WIDTH), jnp.bfloat16),
        compiler_params=pltpu.CompilerParams(
            dimension_semantics=("arbitrary", "arbitrary"), vmem_limit_bytes=VMEM_LIMIT),
        name="gqa",
    )(qa, kva, ga)


def _na_first_key_row(first_query_row, clip):
    return clip(first_query_row - NA_KH // 2, 0, GRID_ROWS - NA_SUB_WIN)


def _na_build_bias(strip_ref, bias_ref):
    kl = lax.broadcasted_iota(jnp.int32, (1, NA_SUB_TK), 1) // GRID_W
    for kind, first_row in enumerate((0, NA_SUB_ROWS, GRID_ROWS - NA_SUB_ROWS)):
        win0 = int(_na_first_key_row(first_row, np.clip))
        for a in range(NA_SUB_ROWS):
            qr = first_row + a
            r0 = int(np.clip(qr - NA_KH // 2, 0, GRID_ROWS - NA_KH))
            valid = (kl >= r0 - win0) & (kl < r0 - win0 + NA_KH)
            t0 = win0 - qr + (NA_STRIP_BLOCKS // 2 - 1)
            par = t0 % 2
            c0 = (t0 - par) * GRID_W
            for hh in range(2):
                tile = strip_ref[hh, par, :, c0:c0 + NA_SUB_TK]
                bias_ref[hh, kind, a * GRID_W:(a + 1) * GRID_W, :] = jnp.where(valid, tile, MASK_VALUE)


def _natten_kernel(q_ref, k_ref, v_ref, g_ref, strip_ref, o_ref, bias_ref):
    i = pl.program_id(1)
    bi = pl.program_id(2)
    n_sub = GRID_ROWS // NA_SUB_ROWS

    @pl.when((i == 0) & (bi == 0))
    def _():
        _na_build_bias(strip_ref, bias_ref)

    for half in range(NA_ROWS // NA_SUB_ROWS):
        sub = i * (NA_ROWS // NA_SUB_ROWS) + half
        kind = jnp.where(sub == 0, 0, jnp.where(sub == n_sub - 1, 2, 1))
        start = pl.multiple_of(_na_first_key_row(sub * NA_SUB_ROWS, jnp.clip) * GRID_W, NA_SUB_ROWS * GRID_W)
        rows = slice(half * NA_SUB_TQ, (half + 1) * NA_SUB_TQ)
        k = k_ref[0, pl.ds(start, NA_SUB_TK), :]
        v = v_ref[0, pl.ds(start, NA_SUB_TK), :]
        o = _pair_attention(q_ref[0, rows, :], k, v, (bias_ref[0, kind], bias_ref[1, kind]))
        o_ref[0, rows, :] = (o * _silu(g_ref[0, rows, :])).astype(jnp.bfloat16)


def _natten(qkvb, gb, strips):
    b, s, _ = qkvb.shape
    n_pairs = B_WIDTH // LANES
    qtile = lambda p, i, bi: (bi, i, p)
    return pl.pallas_call(
        _natten_kernel,
        grid=(n_pairs, s // NA_TQ, b),
        in_specs=[
            pl.BlockSpec((1, NA_TQ, LANES), qtile),
            pl.BlockSpec((1, s, LANES), lambda p, i, bi: (bi, 0, n_pairs + p)),
            pl.BlockSpec((1, s, LANES), lambda p, i, bi: (bi, 0, 2 * n_pairs + p)),
            pl.BlockSpec((1, NA_TQ, LANES), qtile),
            pl.BlockSpec((2, 2, GRID_W, NA_STRIP_BLOCKS * GRID_W), lambda p, i, bi: (p, 0, 0, 0)),
        ],
        out_specs=pl.BlockSpec((1, NA_TQ, LANES), qtile),
        out_shape=jax.ShapeDtypeStruct((b, s, B_WIDTH), jnp.bfloat16),
        scratch_shapes=[pltpu.VMEM((2, 3, NA_SUB_TQ, NA_SUB_TK), jnp.float32)],
        compiler_params=pltpu.CompilerParams(
            dimension_semantics=("arbitrary", "arbitrary", "arbitrary"), vmem_limit_bytes=VMEM_LIMIT),
        name="natten",
    )(qkvb, qkvb, qkvb, gb, strips)


def _na_bias_strips(rpb):
    c = np.arange(GRID_W)
    col_start = np.clip(c - NA_KW // 2, 0, GRID_W - NA_KW)
    kc = c[None, :]
    col_valid = (kc >= col_start[:, None]) & (kc < col_start[:, None] + NA_KW)
    col_idx = np.clip(kc - c[:, None] + (NA_KW - 1), 0, 2 * NA_KW - 2)
    bc = jnp.where(col_valid[None, None], rpb[:, :, col_idx] * LOG2E, MASK_VALUE)
    n_dr = 2 * NA_KH - 1
    flat = bc.transpose(0, 2, 1, 3).reshape(B_HEADS, GRID_W, n_dr * GRID_W)
    left = (NA_STRIP_BLOCKS // 2 - 1) - (NA_KH - 1)
    right = NA_STRIP_BLOCKS - left - n_dr
    strip = jnp.pad(flat, ((0, 0), (0, 0), (left * GRID_W, right * GRID_W)))
    shifted = jnp.pad(strip[:, :, GRID_W:], ((0, 0), (0, 0), (0, GRID_W)))
    return jnp.stack([strip, shifted], axis=1)


def _out_proj_kernel(x_ref, ma_ref, mb_ref, w_ref, gain_ref, o_ref):
    y = x_ref[0]
    y = y + jnp.dot(ma_ref[0], w_ref[:A_WIDTH, :], preferred_element_type=jnp.float32)
    y = y + jnp.dot(mb_ref[0], w_ref[A_WIDTH:, :], preferred_element_type=jnp.float32)
    ms = jnp.mean(y * y, axis=-1, keepdims=True)
    o_ref[0] = (y * lax.rsqrt(ms + EPS)) * gain_ref[...]


def _out_proj(x, ma, mb, w, gain):
    b, s, d = x.shape
    row = lambda bi, si: (bi, si, 0)
    const2 = lambda bi, si: (0, 0)
    return pl.pallas_call(
        _out_proj_kernel,
        grid=(b, s // OUT_TM),
        in_specs=[
            pl.BlockSpec((1, OUT_TM, d), row),
            pl.BlockSpec((1, OUT_TM, A_WIDTH), row),
            pl.BlockSpec((1, OUT_TM, B_WIDTH), row),
            pl.BlockSpec((A_WIDTH + B_WIDTH, d), const2),
            pl.BlockSpec((1, d), const2),
        ],
        out_specs=pl.BlockSpec((1, OUT_TM, d), row),
        out_shape=jax.ShapeDtypeStruct((b, s, d), jnp.float32),
        compiler_params=pltpu.CompilerParams(
            dimension_semantics=("arbitrary", "arbitrary"), vmem_limit_bytes=VMEM_LIMIT),
        name="out_proj",
    )(x, ma, mb, w, gain)


def _rope_tables(seq):
    t = np.arange(seq)
    inv = ROPE_THETA ** (-np.arange(ROPE_PAIRS, dtype=np.float64) * (2.0 / (HEAD_DIM // 2)))
    ang_r = (t // GRID_W)[:, None] * inv[None, :]
    ang_c = (t % GRID_W)[:, None] * inv[None, :]
    ang = np.concatenate([ang_r, ang_r, ang_c, ang_c], axis=-1)
    sign = np.where(np.arange(HEAD_DIM) % (2 * ROPE_PAIRS) < ROPE_PAIRS, -1.0, 1.0)
    reps = (1, LANES // HEAD_DIM)
    cos = np.tile(np.cos(ang), reps).astype(np.float32)
    sin_signed = np.tile(np.sin(ang) * sign, reps).astype(np.float32)
    return jnp.asarray(cos), jnp.asarray(sin_signed)


def kernel(x, norm_gain, w_in, q_norm_a, k_norm_a, na_rpb, w_out, final_norm_gain):
    assert w_in.shape[0] == 1, "single-layer trunk only"
    assert x.shape[1:] == (SEQ, D_MODEL) and w_in.shape[2] == C_END
    cos, sin_signed = _rope_tables(SEQ)
    reps = LANES // HEAD_DIM
    qgain = jnp.tile(q_norm_a[0] * Q_SCALE, reps)[None, :]
    kgain = jnp.tile(k_norm_a[0], reps)[None, :]
    qa, kva, ga, gb, qkvb = _in_proj(x, norm_gain, w_in[0].astype(jnp.bfloat16), qgain, kgain, cos, sin_signed)
    mixed_a = _gqa(qa, kva, ga)
    mixed_b = _natten(qkvb, gb, _na_bias_strips(na_rpb[0]))
    return _out_proj(x, mixed_a, mixed_b, w_out[0].astype(jnp.bfloat16), final_norm_gain[None, :])
```

```python
import math

import jax
import jax.numpy as jnp
import numpy as np
from jax import lax
from jax.experimental import pallas as pl
from jax.experimental.pallas import tpu as pltpu

D_MODEL = 1024
SEQ = 2048
GRID_W = 64
GRID_ROWS = SEQ // GRID_W
HEAD_DIM = 64
A_HEADS = 8
A_KV_HEADS = 2
B_HEADS = 8
A_WIDTH = A_HEADS * HEAD_DIM
A_KV_WIDTH = A_KV_HEADS * HEAD_DIM
B_WIDTH = B_HEADS * HEAD_DIM
ROPE_PAIRS = HEAD_DIM // 4
ROPE_THETA = 10000.0
NA_KH = 8
NA_KW = 16
EPS = 1e-6
LOG2E = math.log2(math.e)
Q_SCALE = HEAD_DIM ** -0.5 * LOG2E

LANES = 128
MASK_VALUE = -1e30

IN_TM = 512
GQA_TQ = 1024
GQA_SUB_TQ = 256
NA_SUB_ROWS = 4
NA_SUB_WIN = NA_SUB_ROWS + NA_KH
NA_SUB_TQ = NA_SUB_ROWS * GRID_W
NA_SUB_TK = NA_SUB_WIN * GRID_W
NA_DR0 = 11
NA_DR_ROWS = 24
OUT_TM = 512
VMEM_LIMIT = 48 * 1024 * 1024

C_QA = 0
C_KA = C_QA + A_WIDTH
C_GA = C_KA + 2 * A_KV_WIDTH
C_QB = C_GA + A_WIDTH
C_KB = C_QB + B_WIDTH
C_GB = C_KB + 2 * B_WIDTH
C_END = C_GB + B_WIDTH


def _lane_lo():
    return lax.broadcasted_iota(jnp.int32, (1, LANES), 1) < HEAD_DIM


def _silu(g):
    return g * (1.0 / (1.0 + jnp.exp(-g)))


def _head_norm_rope(y, gain, cos, sin_signed):
    lo = _lane_lo()
    y2 = y * y
    s_lo = jnp.sum(jnp.where(lo, y2, 0.0), axis=-1, keepdims=True)
    s_hi = jnp.sum(jnp.where(lo, 0.0, y2), axis=-1, keepdims=True)
    ms = jnp.where(lo, s_lo, s_hi) * (1.0 / HEAD_DIM)
    yn = (y * lax.rsqrt(ms + EPS)) * gain
    lane = lax.broadcasted_iota(jnp.int32, (1, LANES), 1)
    first_half = lane % (2 * ROPE_PAIRS) < ROPE_PAIRS
    nxt = pltpu.roll(yn, LANES - ROPE_PAIRS, 1)
    prv = pltpu.roll(yn, ROPE_PAIRS, 1)
    rot = jnp.where(first_half, nxt, prv)
    return yn * cos + rot * sin_signed


def _dup_heads(y):
    lo = _lane_lo()
    swapped = pltpu.roll(y, HEAD_DIM, 1)
    return jnp.where(lo, y, swapped), jnp.where(lo, swapped, y)


def _in_proj_kernel(x_ref, gain_ref, w_ref, qgain_ref, kgain_ref, cos_ref, sin_ref,
                    qa_ref, kva_ref, ga_ref, gb_ref, qkvb_ref):
    x = x_ref[0]
    ms = jnp.mean(x * x, axis=-1, keepdims=True)
    h = ((x * lax.rsqrt(ms + EPS)) * gain_ref[...]).astype(jnp.bfloat16)
    cos = cos_ref[...]
    sin = sin_ref[...]
    bf16 = jnp.bfloat16

    qkva = jnp.dot(h, w_ref[:, C_QA:C_GA], preferred_element_type=jnp.float32)
    for j in range(A_WIDTH // LANES):
        sl = slice(j * LANES, (j + 1) * LANES)
        qa_ref[0, :, sl] = _head_norm_rope(qkva[:, sl], qgain_ref[...], cos, sin).astype(bf16)
    ka = _head_norm_rope(qkva[:, C_KA:C_KA + LANES], kgain_ref[...], cos, sin)
    va = qkva[:, C_KA + LANES:C_GA]
    for n, slab in enumerate(_dup_heads(ka) + _dup_heads(va)):
        kva_ref[0, :, n * LANES:(n + 1) * LANES] = slab.astype(bf16)

    ga_ref[0] = jnp.dot(h, w_ref[:, C_GA:C_QB], preferred_element_type=jnp.float32)
    qb = jnp.dot(h, w_ref[:, C_QB:C_KB], preferred_element_type=jnp.float32)
    qkvb_ref[0, :, :B_WIDTH] = (qb * Q_SCALE).astype(bf16)
    qkvb_ref[0, :, B_WIDTH:] = jnp.dot(h, w_ref[:, C_KB:C_GB], preferred_element_type=jnp.float32).astype(bf16)
    gb_ref[0] = jnp.dot(h, w_ref[:, C_GB:C_END], preferred_element_type=jnp.float32)


def _in_proj(x, gain, w, qgain, kgain, cos, sin_signed):
    b, s, d = x.shape
    row = lambda bi, si: (bi, si, 0)
    const2 = lambda bi, si: (0, 0)
    tab = lambda bi, si: (si, 0)
    widths = (A_WIDTH, 4 * LANES, A_WIDTH, B_WIDTH, 3 * B_WIDTH)
    dtypes = (jnp.bfloat16, jnp.bfloat16, jnp.float32, jnp.float32, jnp.bfloat16)
    return pl.pallas_call(
        _in_proj_kernel,
        grid=(b, s // IN_TM),
        in_specs=[
            pl.BlockSpec((1, IN_TM, d), row),
            pl.BlockSpec((1, d), const2),
            pl.BlockSpec((d, C_END), const2),
            pl.BlockSpec((1, LANES), const2),
            pl.BlockSpec((1, LANES), const2),
            pl.BlockSpec((IN_TM, LANES), tab),
            pl.BlockSpec((IN_TM, LANES), tab),
        ],
        out_specs=[pl.BlockSpec((1, IN_TM, n), row) for n in widths],
        out_shape=[jax.ShapeDtypeStruct((b, s, n), t) for n, t in zip(widths, dtypes)],
        compiler_params=pltpu.CompilerParams(
            dimension_semantics=("arbitrary", "arbitrary"), vmem_limit_bytes=VMEM_LIMIT),
        name="in_proj",
    )(x, gain, w, qgain, kgain, cos, sin_signed)


def _pair_attention(q, k, v, bias):
    lo = _lane_lo()
    tq = q.shape[0]
    one = jnp.ones((), v.dtype)
    zero = jnp.zeros((), q.dtype)
    qs = jnp.concatenate([jnp.where(lo, q, zero), jnp.where(lo, zero, q)], axis=0)
    vs = jnp.concatenate([jnp.where(lo, v, one), jnp.where(lo, one, v)], axis=1)
    s = lax.dot_general(qs, k, (((1,), (1,)), ((), ())), preferred_element_type=jnp.float32)
    if bias is not None:
        s = s + bias
    m = jnp.max(s, axis=-1, keepdims=True)
    p = jnp.exp2(s - m).astype(jnp.bfloat16)
    o2 = jnp.dot(p, vs, preferred_element_type=jnp.float32)
    a = o2[:tq, :LANES]
    b = o2[tq:, LANES:]
    numer = jnp.where(lo, a, b)
    denom = jnp.where(lo, pltpu.roll(a, HEAD_DIM, 1), pltpu.roll(b, HEAD_DIM, 1))
    return numer / denom


def _gqa_kernel(qa_ref, kva_ref, g_ref, o_ref):
    pairs_per_kv = (A_HEADS // A_KV_HEADS) // 2
    for sub in range(GQA_TQ // GQA_SUB_TQ):
        rows = slice(sub * GQA_SUB_TQ, (sub + 1) * GQA_SUB_TQ)
        for j in range(A_WIDTH // LANES):
            kv = j // pairs_per_kv
            k = kva_ref[0, :, kv * LANES:(kv + 1) * LANES]
            v = kva_ref[0, :, (A_KV_HEADS + kv) * LANES:(A_KV_HEADS + kv + 1) * LANES]
            sl = slice(j * LANES, (j + 1) * LANES)
            o = _pair_attention(qa_ref[0, rows, sl], k, v, None)
            o_ref[0, rows, sl] = (o * _silu(g_ref[0, rows, sl])).astype(jnp.bfloat16)


def _gqa(qa, kva, ga):
    b, s, _ = qa.shape
    tile = lambda bi, qi: (bi, qi, 0)
    return pl.pallas_call(
        _gqa_kernel,
        grid=(b, s // GQA_TQ),
        in_specs=[
            pl.BlockSpec((1, GQA_TQ, A_WIDTH), tile),
            pl.BlockSpec((1, s, kva.shape[2]), lambda bi, qi: (bi, 0, 0)),
            pl.BlockSpec((1, GQA_TQ, A_WIDTH), tile),
        ],
        out_specs=pl.BlockSpec((1, GQA_TQ, A_WIDTH), tile),
        out_shape=jax.ShapeDtypeStruct((b, s, A_WIDTH), jnp.bfloat16),
        compiler_params=pltpu.CompilerParams(
            dimension_semantics=("arbitrary", "arbitrary"), vmem_limit_bytes=VMEM_LIMIT),
        name="gqa",
    )(qa, kva, ga)


def _na_first_key_row(first_query_row):
    return min(max(first_query_row - NA_KH // 2, 0), GRID_ROWS - NA_SUB_WIN)


def _na_build_bias(rpb_ref, bias_ref):
    shape = (GRID_W, LANES)
    qc = lax.broadcasted_iota(jnp.int32, shape, 0)
    lane = lax.broadcasted_iota(jnp.int32, shape, 1)
    kc = lane % GRID_W
    col_start = jnp.clip(qc - NA_KW // 2, 0, GRID_W - NA_KW)
    col_valid = (kc >= col_start) & (kc < col_start + NA_KW)
    valid = {(True, True): col_valid,
             (True, False): col_valid & (lane < GRID_W),
             (False, True): col_valid & (lane >= GRID_W)}
    for hh in range(2):
        tiles = {}

        def pair_tile(dr):
            if dr not in tiles:
                row = jnp.broadcast_to(rpb_ref[hh, dr + NA_DR0:dr + NA_DR0 + 1, :], shape)
                tiles[dr] = pltpu.roll(row, LANES - (NA_KW - 1), 1, stride=1, stride_axis=0)
            return tiles[dr]

        for kind, first_row in enumerate((0, NA_SUB_ROWS, GRID_ROWS - NA_SUB_ROWS)):
            win0 = _na_first_key_row(first_row)
            for a in range(NA_SUB_ROWS):
                qr = first_row + a
                r0 = min(max(qr - NA_KH // 2, 0), GRID_ROWS - NA_KH)
                r = hh * NA_SUB_TQ + a * GRID_W
                for kl in range(0, NA_SUB_WIN, 2):
                    row_valid = tuple(r0 <= win0 + kl + d < r0 + NA_KH for d in range(2))
                    if row_valid == (False, False):
                        tile = jnp.full(shape, MASK_VALUE, jnp.float32)
                    else:
                        tile = jnp.where(valid[row_valid], pair_tile(win0 + kl - qr), MASK_VALUE)
                    bias_ref[kind, r:r + GRID_W, kl * GRID_W:(kl + 2) * GRID_W] = tile


def _natten_kernel(q_ref, k_ref, v_ref, g_ref, rpb_ref, o_ref, bias_ref):
    @pl.when(pl.program_id(1) == 0)
    def _():
        _na_build_bias(rpb_ref, bias_ref)

    n_sub = GRID_ROWS // NA_SUB_ROWS
    for sub in range(n_sub):
        kind = 0 if sub == 0 else (2 if sub == n_sub - 1 else 1)
        start = _na_first_key_row(sub * NA_SUB_ROWS) * GRID_W
        rows = slice(sub * NA_SUB_TQ, (sub + 1) * NA_SUB_TQ)
        k = k_ref[0, start:start + NA_SUB_TK, :]
        v = v_ref[0, start:start + NA_SUB_TK, :]
        o = _pair_attention(q_ref[0, rows, :], k, v, bias_ref[kind])
        o_ref[0, rows, :] = (o * _silu(g_ref[0, rows, :])).astype(jnp.bfloat16)


def _natten(qkvb, gb, rpb_pairs):
    b, s, _ = qkvb.shape
    n_pairs = B_WIDTH // LANES
    col = lambda c: (lambda p, bi: (bi, 0, c * n_pairs + p))
    return pl.pallas_call(
        _natten_kernel,
        grid=(n_pairs, b),
        in_specs=[
            pl.BlockSpec((1, s, LANES), col(0)),
            pl.BlockSpec((1, s, LANES), col(1)),
            pl.BlockSpec((1, s, LANES), col(2)),
            pl.BlockSpec((1, s, LANES), col(0)),
            pl.BlockSpec((2, NA_DR_ROWS, LANES), lambda p, bi: (p, 0, 0)),
        ],
        out_specs=pl.BlockSpec((1, s, LANES), col(0)),
        out_shape=jax.ShapeDtypeStruct((b, s, B_WIDTH), jnp.bfloat16),
        scratch_shapes=[pltpu.VMEM((3, 2 * NA_SUB_TQ, NA_SUB_TK), jnp.float32)],
        compiler_params=pltpu.CompilerParams(
            dimension_semantics=("arbitrary", "arbitrary"), vmem_limit_bytes=VMEM_LIMIT),
        name="natten",
    )(qkvb, qkvb, qkvb, gb, rpb_pairs)


def _na_rpb_pairs(rpb):
    n_dr = 2 * NA_KH - 1
    half = jnp.pad(rpb * LOG2E, ((0, 0), (0, 0), (0, GRID_W - rpb.shape[2])))
    lo_pad = NA_DR0 - (NA_KH - 1)
    ext = jnp.pad(half, ((0, 0), (lo_pad, NA_DR_ROWS + 1 - lo_pad - n_dr), (0, 0)))
    return jnp.concatenate([ext[:, :-1], ext[:, 1:]], axis=-1)


def _out_proj_kernel(x_ref, ma_ref, mb_ref, w_ref, gain_ref, o_ref):
    y = x_ref[0]
    y = y + jnp.dot(ma_ref[0], w_ref[:A_WIDTH, :], preferred_element_type=jnp.float32)
    y = y + jnp.dot(mb_ref[0], w_ref[A_WIDTH:, :], preferred_element_type=jnp.float32)
    ms = jnp.mean(y * y, axis=-1, keepdims=True)
    o_ref[0] = (y * lax.rsqrt(ms + EPS)) * gain_ref[...]


def _out_proj(x, ma, mb, w, gain):
    b, s, d = x.shape
    row = lambda bi, si: (bi, si, 0)
    const2 = lambda bi, si: (0, 0)
    return pl.pallas_call(
        _out_proj_kernel,
        grid=(b, s // OUT_TM),
        in_specs=[
            pl.BlockSpec((1, OUT_TM, d), row),
            pl.BlockSpec((1, OUT_TM, A_WIDTH), row),
            pl.BlockSpec((1, OUT_TM, B_WIDTH), row),
            pl.BlockSpec((A_WIDTH + B_WIDTH, d), const2),
            pl.BlockSpec((1, d), const2),
        ],
        out_specs=pl.BlockSpec((1, OUT_TM, d), row),
        out_shape=jax.ShapeDtypeStruct((b, s, d), jnp.float32),
        compiler_params=pltpu.CompilerParams(
            dimension_semantics=("arbitrary", "arbitrary"), vmem_limit_bytes=VMEM_LIMIT),
        name="out_proj",
    )(x, ma, mb, w, gain)


def _rope_tables(seq):
    t = np.arange(seq)
    inv = ROPE_THETA ** (-np.arange(ROPE_PAIRS, dtype=np.float64) * (2.0 / (HEAD_DIM // 2)))
    ang_r = (t // GRID_W)[:, None] * inv[None, :]
    ang_c = (t % GRID_W)[:, None] * inv[None, :]
    ang = np.concatenate([ang_r, ang_r, ang_c, ang_c], axis=-1)
    sign = np.where(np.arange(HEAD_DIM) % (2 * ROPE_PAIRS) < ROPE_PAIRS, -1.0, 1.0)
    reps = (1, LANES // HEAD_DIM)
    cos = np.tile(np.cos(ang), reps).astype(np.float32)
    sin_signed = np.tile(np.sin(ang) * sign, reps).astype(np.float32)
    return jnp.asarray(cos), jnp.asarray(sin_signed)


def kernel(x, norm_gain, w_in, q_norm_a, k_norm_a, na_rpb, w_out, final_norm_gain):
    assert w_in.shape[0] == 1, "single-layer trunk only"
    assert x.shape[1:] == (SEQ, D_MODEL) and w_in.shape[2] == C_END
    cos, sin_signed = _rope_tables(SEQ)
    reps = LANES // HEAD_DIM
    qgain = jnp.tile(q_norm_a[0] * Q_SCALE, reps)[None, :]
    kgain = jnp.tile(k_norm_a[0], reps)[None, :]
    qa, kva, ga, gb, qkvb = _in_proj(x, norm_gain, w_in[0].astype(jnp.bfloat16), qgain, kgain, cos, sin_signed)
    mixed_a = _gqa(qa, kva, ga)
    mixed_b = _natten(qkvb, gb, _na_rpb_pairs(na_rpb[0]))
    return _out_proj(x, mixed_a, mixed_b, w_out[0].astype(jnp.bfloat16), final_norm_gain[None, :])
```

```python
import math

import jax
import jax.numpy as jnp
import numpy as np
from jax import lax
from jax.experimental import pallas as pl
from jax.experimental.pallas import tpu as pltpu

D_MODEL = 1024
SEQ = 2048
GRID_W = 64
GRID_ROWS = SEQ // GRID_W
HEAD_DIM = 64
A_HEADS = 8
A_KV_HEADS = 2
B_HEADS = 8
A_WIDTH = A_HEADS * HEAD_DIM
A_KV_WIDTH = A_KV_HEADS * HEAD_DIM
B_WIDTH = B_HEADS * HEAD_DIM
ROPE_PAIRS = HEAD_DIM // 4
ROPE_THETA = 10000.0
NA_KH = 8
NA_KW = 16
EPS = 1e-6
LOG2E = math.log2(math.e)
Q_SCALE = HEAD_DIM ** -0.5 * LOG2E

LANES = 128
MASK_VALUE = -1e30

IN_TM = 512
GQA_TQ = 1024
GQA_SUB_TQ = 256
GQA_OUT_ROWS = 1024
NA_SUB_ROWS = 4
NA_SUB_WIN = NA_SUB_ROWS + NA_KH
NA_SUB_TQ = NA_SUB_ROWS * GRID_W
NA_SUB_TK = NA_SUB_WIN * GRID_W
NA_DR0 = 11
NA_DR_ROWS = 24
VMEM_LIMIT = 48 * 1024 * 1024
GQA_VMEM_LIMIT = 56 * 1024 * 1024

C_QA = 0
C_KA = C_QA + A_WIDTH
C_GA = C_KA + 2 * A_KV_WIDTH
C_QB = C_GA + A_WIDTH
C_KB = C_QB + B_WIDTH
C_GB = C_KB + 2 * B_WIDTH
C_END = C_GB + B_WIDTH


def _lane_lo():
    return lax.broadcasted_iota(jnp.int32, (1, LANES), 1) < HEAD_DIM


def _silu(g):
    return g * (1.0 / (1.0 + jnp.exp(-g)))


def _head_norm_rope(y, gain, cos, sin_signed):
    lo = _lane_lo()
    y2 = y * y
    s_lo = jnp.sum(jnp.where(lo, y2, 0.0), axis=-1, keepdims=True)
    s_hi = jnp.sum(jnp.where(lo, 0.0, y2), axis=-1, keepdims=True)
    ms = jnp.where(lo, s_lo, s_hi) * (1.0 / HEAD_DIM)
    yn = (y * lax.rsqrt(ms + EPS)) * gain
    lane = lax.broadcasted_iota(jnp.int32, (1, LANES), 1)
    first_half = lane % (2 * ROPE_PAIRS) < ROPE_PAIRS
    nxt = pltpu.roll(yn, LANES - ROPE_PAIRS, 1)
    prv = pltpu.roll(yn, ROPE_PAIRS, 1)
    rot = jnp.where(first_half, nxt, prv)
    return yn * cos + rot * sin_signed


def _dup_heads(y):
    lo = _lane_lo()
    swapped = pltpu.roll(y, HEAD_DIM, 1)
    return jnp.where(lo, y, swapped), jnp.where(lo, swapped, y)


def _in_proj_kernel(x_ref, gain_ref, w_ref, qgain_ref, kgain_ref, cos_ref, sin_ref,
                    qa_ref, kva_ref, ga_ref, gb_ref, qkvb_ref):
    x = x_ref[0]
    ms = jnp.mean(x * x, axis=-1, keepdims=True)
    h = ((x * lax.rsqrt(ms + EPS)) * gain_ref[...]).astype(jnp.bfloat16)
    cos = cos_ref[...]
    sin = sin_ref[...]
    bf16 = jnp.bfloat16

    qkva = jnp.dot(h, w_ref[:, C_QA:C_GA], preferred_element_type=jnp.float32)
    for j in range(A_WIDTH // LANES):
        sl = slice(j * LANES, (j + 1) * LANES)
        qa_ref[0, :, sl] = _head_norm_rope(qkva[:, sl], qgain_ref[...], cos, sin).astype(bf16)
    ka = _head_norm_rope(qkva[:, C_KA:C_KA + LANES], kgain_ref[...], cos, sin)
    va = qkva[:, C_KA + LANES:C_GA]
    for n, slab in enumerate(_dup_heads(ka) + _dup_heads(va)):
        kva_ref[0, :, n * LANES:(n + 1) * LANES] = slab.astype(bf16)

    ga_ref[0] = jnp.dot(h, w_ref[:, C_GA:C_QB], preferred_element_type=jnp.float32)
    qb = jnp.dot(h, w_ref[:, C_QB:C_KB], preferred_element_type=jnp.float32)
    qkvb_ref[0, :, :B_WIDTH] = (qb * Q_SCALE).astype(bf16)
    qkvb_ref[0, :, B_WIDTH:] = jnp.dot(h, w_ref[:, C_KB:C_GB], preferred_element_type=jnp.float32).astype(bf16)
    gb_ref[0] = jnp.dot(h, w_ref[:, C_GB:C_END], preferred_element_type=jnp.float32)


def _in_proj(x, gain, w, qgain, kgain, cos, sin_signed):
    b, s, d = x.shape
    row = lambda bi, si: (bi, si, 0)
    const2 = lambda bi, si: (0, 0)
    tab = lambda bi, si: (si, 0)
    widths = (A_WIDTH, 4 * LANES, A_WIDTH, B_WIDTH, 3 * B_WIDTH)
    dtypes = (jnp.bfloat16, jnp.bfloat16, jnp.float32, jnp.float32, jnp.bfloat16)
    return pl.pallas_call(
        _in_proj_kernel,
        grid=(b, s // IN_TM),
        in_specs=[
            pl.BlockSpec((1, IN_TM, d), row),
            pl.BlockSpec((1, d), const2),
            pl.BlockSpec((d, C_END), const2),
            pl.BlockSpec((1, LANES), const2),
            pl.BlockSpec((1, LANES), const2),
            pl.BlockSpec((IN_TM, LANES), tab),
            pl.BlockSpec((IN_TM, LANES), tab),
        ],
        out_specs=[pl.BlockSpec((1, IN_TM, n), row) for n in widths],
        out_shape=[jax.ShapeDtypeStruct((b, s, n), t) for n, t in zip(widths, dtypes)],
        compiler_params=pltpu.CompilerParams(
            dimension_semantics=("arbitrary", "arbitrary"), vmem_limit_bytes=VMEM_LIMIT),
        name="in_proj",
    )(x, gain, w, qgain, kgain, cos, sin_signed)


def _pair_attention(q, k, v, bias):
    lo = _lane_lo()
    tq = q.shape[0]
    one = jnp.ones((), v.dtype)
    zero = jnp.zeros((), q.dtype)
    qs = jnp.concatenate([jnp.where(lo, q, zero), jnp.where(lo, zero, q)], axis=0)
    vs = jnp.concatenate([jnp.where(lo, v, one), jnp.where(lo, one, v)], axis=1)
    s = lax.dot_general(qs, k, (((1,), (1,)), ((), ())), preferred_element_type=jnp.float32)
    if bias is not None:
        s = s + bias
    m = jnp.max(s, axis=-1, keepdims=True)
    p = jnp.exp2(s - m).astype(jnp.bfloat16)
    o2 = jnp.dot(p, vs, preferred_element_type=jnp.float32)
    a = o2[:tq, :LANES]
    b = o2[tq:, LANES:]
    numer = jnp.where(lo, a, b)
    denom = jnp.where(lo, pltpu.roll(a, HEAD_DIM, 1), pltpu.roll(b, HEAD_DIM, 1))
    return numer / denom


def _gqa_out_kernel(qa_ref, kva_ref, g_ref, mb_ref, x_ref, w_ref, gain_ref, o_ref, ma_ref):
    pairs_per_kv = (A_HEADS // A_KV_HEADS) // 2
    subs_per_out = GQA_OUT_ROWS // GQA_SUB_TQ
    for sub in range(GQA_TQ // GQA_SUB_TQ):
        rows = slice(sub * GQA_SUB_TQ, (sub + 1) * GQA_SUB_TQ)
        for j in range(A_WIDTH // LANES):
            kv = j // pairs_per_kv
            k = kva_ref[0, :, kv * LANES:(kv + 1) * LANES]
            v = kva_ref[0, :, (A_KV_HEADS + kv) * LANES:(A_KV_HEADS + kv + 1) * LANES]
            sl = slice(j * LANES, (j + 1) * LANES)
            o = _pair_attention(qa_ref[0, rows, sl], k, v, None)
            ma_ref[rows, sl] = (o * _silu(g_ref[0, rows, sl])).astype(jnp.bfloat16)
        if (sub + 1) % subs_per_out == 0:
            rows = slice((sub + 1) * GQA_SUB_TQ - GQA_OUT_ROWS, (sub + 1) * GQA_SUB_TQ)
            y = x_ref[0, rows, :]
            y = y + jnp.dot(ma_ref[rows, :], w_ref[:A_WIDTH, :], preferred_element_type=jnp.float32)
            y = y + jnp.dot(mb_ref[0, rows, :], w_ref[A_WIDTH:, :], preferred_element_type=jnp.float32)
            ms = jnp.mean(y * y, axis=-1, keepdims=True)
            o_ref[0, rows, :] = (y * lax.rsqrt(ms + EPS)) * gain_ref[...]


def _gqa_out(qa, kva, ga, mb, x, w, gain):
    b, s, d = x.shape
    tile = lambda bi, qi: (bi, qi, 0)
    const2 = lambda bi, qi: (0, 0)
    return pl.pallas_call(
        _gqa_out_kernel,
        grid=(b, s // GQA_TQ),
        in_specs=[
            pl.BlockSpec((1, GQA_TQ, A_WIDTH), tile),
            pl.BlockSpec((1, s, kva.shape[2]), lambda bi, qi: (bi, 0, 0)),
            pl.BlockSpec((1, GQA_TQ, A_WIDTH), tile),
            pl.BlockSpec((1, GQA_TQ, B_WIDTH), tile),
            pl.BlockSpec((1, GQA_TQ, d), tile),
            pl.BlockSpec((A_WIDTH + B_WIDTH, d), const2),
            pl.BlockSpec((1, d), const2),
        ],
        out_specs=pl.BlockSpec((1, GQA_TQ, d), tile),
        out_shape=jax.ShapeDtypeStruct((b, s, d), jnp.float32),
        scratch_shapes=[pltpu.VMEM((GQA_TQ, A_WIDTH), jnp.bfloat16)],
        compiler_params=pltpu.CompilerParams(
            dimension_semantics=("arbitrary", "arbitrary"), vmem_limit_bytes=GQA_VMEM_LIMIT),
        name="gqa_out",
    )(qa, kva, ga, mb, x, w, gain)


def _na_first_key_row(first_query_row):
    return min(max(first_query_row - NA_KH // 2, 0), GRID_ROWS - NA_SUB_WIN)


def _na_build_bias(rpb_ref, bias_ref):
    shape = (GRID_W, LANES)
    qc = lax.broadcasted_iota(jnp.int32, shape, 0)
    lane = lax.broadcasted_iota(jnp.int32, shape, 1)
    kc = lane % GRID_W
    col_start = jnp.clip(qc - NA_KW // 2, 0, GRID_W - NA_KW)
    col_valid = (kc >= col_start) & (kc < col_start + NA_KW)
    valid = {(True, True): col_valid,
             (True, False): col_valid & (lane < GRID_W),
             (False, True): col_valid & (lane >= GRID_W)}
    for hh in range(2):
        tiles = {}

        def pair_tile(dr):
            if dr not in tiles:
                row = jnp.broadcast_to(rpb_ref[hh, dr + NA_DR0:dr + NA_DR0 + 1, :], shape)
                tiles[dr] = pltpu.roll(row, LANES - (NA_KW - 1), 1, stride=1, stride_axis=0)
            return tiles[dr]

        for kind, first_row in enumerate((0, NA_SUB_ROWS, GRID_ROWS - NA_SUB_ROWS)):
            win0 = _na_first_key_row(first_row)
            for a in range(NA_SUB_ROWS):
                qr = first_row + a
                r0 = min(max(qr - NA_KH // 2, 0), GRID_ROWS - NA_KH)
                r = hh * NA_SUB_TQ + a * GRID_W
                for kl in range(0, NA_SUB_WIN, 2):
                    row_valid = tuple(r0 <= win0 + kl + d < r0 + NA_KH for d in range(2))
                    if row_valid == (False, False):
                        tile = jnp.full(shape, MASK_VALUE, jnp.float32)
                    else:
                        tile = jnp.where(valid[row_valid], pair_tile(win0 + kl - qr), MASK_VALUE)
                    bias_ref[kind, r:r + GRID_W, kl * GRID_W:(kl + 2) * GRID_W] = tile


def _natten_kernel(q_ref, k_ref, v_ref, g_ref, rpb_ref, o_ref, bias_ref):
    @pl.when(pl.program_id(1) == 0)
    def _():
        _na_build_bias(rpb_ref, bias_ref)

    n_sub = GRID_ROWS // NA_SUB_ROWS
    for sub in range(n_sub):
        kind = 0 if sub == 0 else (2 if sub == n_sub - 1 else 1)
        start = _na_first_key_row(sub * NA_SUB_ROWS) * GRID_W
        rows = slice(sub * NA_SUB_TQ, (sub + 1) * NA_SUB_TQ)
        k = k_ref[0, start:start + NA_SUB_TK, :]
        v = v_ref[0, start:start + NA_SUB_TK, :]
        o = _pair_attention(q_ref[0, rows, :], k, v, bias_ref[kind])
        o_ref[0, rows, :] = (o * _silu(g_ref[0, rows, :])).astype(jnp.bfloat16)


def _natten(qkvb, gb, rpb_pairs):
    b, s, _ = qkvb.shape
    n_pairs = B_WIDTH // LANES
    col = lambda c: (lambda p, bi: (bi, 0, c * n_pairs + p))
    return pl.pallas_call(
        _natten_kernel,
        grid=(n_pairs, b),
        in_specs=[
            pl.BlockSpec((1, s, LANES), col(0)),
            pl.BlockSpec((1, s, LANES), col(1)),
            pl.BlockSpec((1, s, LANES), col(2)),
            pl.BlockSpec((1, s, LANES), col(0)),
            pl.BlockSpec((2, NA_DR_ROWS, LANES), lambda p, bi: (p, 0, 0)),
        ],
        out_specs=pl.BlockSpec((1, s, LANES), col(0)),
        out_shape=jax.ShapeDtypeStruct((b, s, B_WIDTH), jnp.bfloat16),
        scratch_shapes=[pltpu.VMEM((3, 2 * NA_SUB_TQ, NA_SUB_TK), jnp.float32)],
        compiler_params=pltpu.CompilerParams(
            dimension_semantics=("arbitrary", "arbitrary"), vmem_limit_bytes=VMEM_LIMIT),
        name="natten",
    )(qkvb, qkvb, qkvb, gb, rpb_pairs)


def _na_rpb_pairs(rpb):
    n_dr = 2 * NA_KH - 1
    half = jnp.pad(rpb * LOG2E, ((0, 0), (0, 0), (0, GRID_W - rpb.shape[2])))
    lo_pad = NA_DR0 - (NA_KH - 1)
    ext = jnp.pad(half, ((0, 0), (lo_pad, NA_DR_ROWS + 1 - lo_pad - n_dr), (0, 0)))
    return jnp.concatenate([ext[:, :-1], ext[:, 1:]], axis=-1)


def _rope_tables(seq):
    t = np.arange(seq)
    inv = ROPE_THETA ** (-np.arange(ROPE_PAIRS, dtype=np.float64) * (2.0 / (HEAD_DIM // 2)))
    ang_r = (t // GRID_W)[:, None] * inv[None, :]
    ang_c = (t % GRID_W)[:, None] * inv[None, :]
    ang = np.concatenate([ang_r, ang_r, ang_c, ang_c], axis=-1)
    sign = np.where(np.arange(HEAD_DIM) % (2 * ROPE_PAIRS) < ROPE_PAIRS, -1.0, 1.0)
    reps = (1, LANES // HEAD_DIM)
    cos = np.tile(np.cos(ang), reps).astype(np.float32)
    sin_signed = np.tile(np.sin(ang) * sign, reps).astype(np.float32)
    return jnp.asarray(cos), jnp.asarray(sin_signed)


def kernel(x, norm_gain, w_in, q_norm_a, k_norm_a, na_rpb, w_out, final_norm_gain):
    assert w_in.shape[0] == 1, "single-layer trunk only"
    assert x.shape[1:] == (SEQ, D_MODEL) and w_in.shape[2] == C_END
    cos, sin_signed = _rope_tables(SEQ)
    reps = LANES // HEAD_DIM
    qgain = jnp.tile(q_norm_a[0] * Q_SCALE, reps)[None, :]
    kgain = jnp.tile(k_norm_a[0], reps)[None, :]
    qa, kva, ga, gb, qkvb = _in_proj(x, norm_gain, w_in[0].astype(jnp.bfloat16), qgain, kgain, cos, sin_signed)
    mixed_b = _natten(qkvb, gb, _na_rpb_pairs(na_rpb[0]))
    return _gqa_out(qa, kva, ga, mixed_b, x, w_out[0].astype(jnp.bfloat16), final_norm_gain[None, :])
```

```python
import math

import jax
import jax.numpy as jnp
import numpy as np
from jax import lax
from jax.experimental import pallas as pl
from jax.experimental.pallas import tpu as pltpu

D_MODEL = 1024
SEQ = 2048
GRID_W = 64
GRID_ROWS = SEQ // GRID_W
HEAD_DIM = 64
A_HEADS = 8
A_KV_HEADS = 2
B_HEADS = 8
A_WIDTH = A_HEADS * HEAD_DIM
A_KV_WIDTH = A_KV_HEADS * HEAD_DIM
B_WIDTH = B_HEADS * HEAD_DIM
ROPE_PAIRS = HEAD_DIM // 4
ROPE_THETA = 10000.0
NA_KH = 8
NA_KW = 16
EPS = 1e-6
LOG2E = math.log2(math.e)
Q_SCALE = HEAD_DIM ** -0.5 * LOG2E

LANES = 128
MASK_VALUE = -1e30

IN_TM = 1024
GQA_TQ = 1024
GQA_SUB_TQ = 256
GQA_OUT_ROWS = 1024
NA_SUB_ROWS = 4
NA_SUB_WIN = NA_SUB_ROWS + NA_KH
NA_SUB_TQ = NA_SUB_ROWS * GRID_W
NA_SUB_TK = NA_SUB_WIN * GRID_W
NA_DR0 = 11
NA_DR_ROWS = 24
NA_STEP_PAIRS = 2
VMEM_LIMIT = 48 * 1024 * 1024
GQA_VMEM_LIMIT = 56 * 1024 * 1024

C_QA = 0
C_KA = C_QA + A_WIDTH
C_GA = C_KA + 2 * A_KV_WIDTH
C_QB = C_GA + A_WIDTH
C_KB = C_QB + B_WIDTH
C_GB = C_KB + 2 * B_WIDTH
C_END = C_GB + B_WIDTH


def _lane_lo():
    return lax.broadcasted_iota(jnp.int32, (1, LANES), 1) < HEAD_DIM


def _silu(g):
    return g * (1.0 / (1.0 + jnp.exp(-g)))


def _head_norm_rope(y, gain, cos, sin_signed):
    lo = _lane_lo()
    y2 = y * y
    s_lo = jnp.sum(jnp.where(lo, y2, 0.0), axis=-1, keepdims=True)
    s_hi = jnp.sum(jnp.where(lo, 0.0, y2), axis=-1, keepdims=True)
    ms = jnp.where(lo, s_lo, s_hi) * (1.0 / HEAD_DIM)
    yn = (y * lax.rsqrt(ms + EPS)) * gain
    lane = lax.broadcasted_iota(jnp.int32, (1, LANES), 1)
    first_half = lane % (2 * ROPE_PAIRS) < ROPE_PAIRS
    nxt = pltpu.roll(yn, LANES - ROPE_PAIRS, 1)
    prv = pltpu.roll(yn, ROPE_PAIRS, 1)
    rot = jnp.where(first_half, nxt, prv)
    return yn * cos + rot * sin_signed


def _dup_heads(y):
    lo = _lane_lo()
    swapped = pltpu.roll(y, HEAD_DIM, 1)
    return jnp.where(lo, y, swapped), jnp.where(lo, swapped, y)


def _in_proj_kernel(x_ref, gain_ref, w_ref, qgain_ref, kgain_ref, cos_ref, sin_ref,
                    qa_ref, kva_ref, ga_ref, gb_ref, qkvb_ref):
    x = x_ref[0]
    ms = jnp.mean(x * x, axis=-1, keepdims=True)
    h = ((x * lax.rsqrt(ms + EPS)) * gain_ref[...]).astype(jnp.bfloat16)
    cos = cos_ref[...]
    sin = sin_ref[...]
    bf16 = jnp.bfloat16

    qkva = jnp.dot(h, w_ref[:, C_QA:C_GA], preferred_element_type=jnp.float32)
    for j in range(A_WIDTH // LANES):
        sl = slice(j * LANES, (j + 1) * LANES)
        qa_ref[0, :, sl] = _head_norm_rope(qkva[:, sl], qgain_ref[...], cos, sin).astype(bf16)
    ka = _head_norm_rope(qkva[:, C_KA:C_KA + LANES], kgain_ref[...], cos, sin)
    va = qkva[:, C_KA + LANES:C_GA]
    for n, slab in enumerate(_dup_heads(ka) + _dup_heads(va)):
        kva_ref[0, :, n * LANES:(n + 1) * LANES] = slab.astype(bf16)

    ga_ref[0] = jnp.dot(h, w_ref[:, C_GA:C_QB], preferred_element_type=jnp.float32)
    qb = jnp.dot(h, w_ref[:, C_QB:C_KB], preferred_element_type=jnp.float32)
    qkvb_ref[0, :, :B_WIDTH] = (qb * Q_SCALE).astype(bf16)
    qkvb_ref[0, :, B_WIDTH:] = jnp.dot(h, w_ref[:, C_KB:C_GB], preferred_element_type=jnp.float32).astype(bf16)
    gb_ref[0] = jnp.dot(h, w_ref[:, C_GB:C_END], preferred_element_type=jnp.float32)


def _in_proj(x, gain, w, qgain, kgain, cos, sin_signed):
    b, s, d = x.shape
    row = lambda bi, si: (bi, si, 0)
    const2 = lambda bi, si: (0, 0)
    tab = lambda bi, si: (si, 0)
    widths = (A_WIDTH, 4 * LANES, A_WIDTH, B_WIDTH, 3 * B_WIDTH)
    dtypes = (jnp.bfloat16, jnp.bfloat16, jnp.float32, jnp.float32, jnp.bfloat16)
    return pl.pallas_call(
        _in_proj_kernel,
        grid=(b, s // IN_TM),
        in_specs=[
            pl.BlockSpec((1, IN_TM, d), row),
            pl.BlockSpec((1, d), const2),
            pl.BlockSpec((d, C_END), const2),
            pl.BlockSpec((1, LANES), const2),
            pl.BlockSpec((1, LANES), const2),
            pl.BlockSpec((IN_TM, LANES), tab),
            pl.BlockSpec((IN_TM, LANES), tab),
        ],
        out_specs=[pl.BlockSpec((1, IN_TM, n), row) for n in widths],
        out_shape=[jax.ShapeDtypeStruct((b, s, n), t) for n, t in zip(widths, dtypes)],
        compiler_params=pltpu.CompilerParams(
            dimension_semantics=("arbitrary", "arbitrary"), vmem_limit_bytes=VMEM_LIMIT),
        name="in_proj",
    )(x, gain, w, qgain, kgain, cos, sin_signed)


def _pair_attention(q, k, v, bias):
    lo = _lane_lo()
    tq = q.shape[0]
    one = jnp.ones((), v.dtype)
    zero = jnp.zeros((), q.dtype)
    qs = jnp.concatenate([jnp.where(lo, q, zero), jnp.where(lo, zero, q)], axis=0)
    vs = jnp.concatenate([jnp.where(lo, v, one), jnp.where(lo, one, v)], axis=1)
    s = lax.dot_general(qs, k, (((1,), (1,)), ((), ())), preferred_element_type=jnp.float32)
    if bias is not None:
        s = s + bias
    m = jnp.max(s, axis=-1, keepdims=True)
    p = jnp.exp2(s - m).astype(jnp.bfloat16)
    o2 = jnp.dot(p, vs, preferred_element_type=jnp.float32)
    a = o2[:tq, :LANES]
    b = o2[tq:, LANES:]
    numer = jnp.where(lo, a, b)
    denom = jnp.where(lo, pltpu.roll(a, HEAD_DIM, 1), pltpu.roll(b, HEAD_DIM, 1))
    return numer / denom


def _gqa_out_kernel(qa_ref, kva_ref, g_ref, mb_ref, x_ref, w_ref, gain_ref, o_ref, ma_ref):
    pairs_per_kv = (A_HEADS // A_KV_HEADS) // 2
    subs_per_out = GQA_OUT_ROWS // GQA_SUB_TQ
    for sub in range(GQA_TQ // GQA_SUB_TQ):
        rows = slice(sub * GQA_SUB_TQ, (sub + 1) * GQA_SUB_TQ)
        for j in range(A_WIDTH // LANES):
            kv = j // pairs_per_kv
            k = kva_ref[0, :, kv * LANES:(kv + 1) * LANES]
            v = kva_ref[0, :, (A_KV_HEADS + kv) * LANES:(A_KV_HEADS + kv + 1) * LANES]
            sl = slice(j * LANES, (j + 1) * LANES)
            o = _pair_attention(qa_ref[0, rows, sl], k, v, None)
            ma_ref[rows, sl] = (o * _silu(g_ref[0, rows, sl])).astype(jnp.bfloat16)
        if (sub + 1) % subs_per_out == 0:
            rows = slice((sub + 1) * GQA_SUB_TQ - GQA_OUT_ROWS, (sub + 1) * GQA_SUB_TQ)
            y = x_ref[0, rows, :]
            y = y + jnp.dot(ma_ref[rows, :], w_ref[:A_WIDTH, :], preferred_element_type=jnp.float32)
            y = y + jnp.dot(mb_ref[0, rows, :], w_ref[A_WIDTH:, :], preferred_element_type=jnp.float32)
            ms = jnp.mean(y * y, axis=-1, keepdims=True)
            o_ref[0, rows, :] = (y * lax.rsqrt(ms + EPS)) * gain_ref[...]


def _gqa_out(qa, kva, ga, mb, x, w, gain):
    b, s, d = x.shape
    tile = lambda bi, qi: (bi, qi, 0)
    const2 = lambda bi, qi: (0, 0)
    return pl.pallas_call(
        _gqa_out_kernel,
        grid=(b, s // GQA_TQ),
        in_specs=[
            pl.BlockSpec((1, GQA_TQ, A_WIDTH), tile),
            pl.BlockSpec((1, s, kva.shape[2]), lambda bi, qi: (bi, 0, 0)),
            pl.BlockSpec((1, GQA_TQ, A_WIDTH), tile),
            pl.BlockSpec((1, GQA_TQ, B_WIDTH), tile),
            pl.BlockSpec((1, GQA_TQ, d), tile),
            pl.BlockSpec((A_WIDTH + B_WIDTH, d), const2),
            pl.BlockSpec((1, d), const2),
        ],
        out_specs=pl.BlockSpec((1, GQA_TQ, d), tile),
        out_shape=jax.ShapeDtypeStruct((b, s, d), jnp.float32),
        scratch_shapes=[pltpu.VMEM((GQA_TQ, A_WIDTH), jnp.bfloat16)],
        compiler_params=pltpu.CompilerParams(
            dimension_semantics=("arbitrary", "arbitrary"), vmem_limit_bytes=GQA_VMEM_LIMIT),
        name="gqa_out",
    )(qa, kva, ga, mb, x, w, gain)


def _na_first_key_row(first_query_row):
    return min(max(first_query_row - NA_KH // 2, 0), GRID_ROWS - NA_SUB_WIN)


def _na_build_bias(rpb_ref, bias_ref):
    shape = (GRID_W, LANES)
    qc = lax.broadcasted_iota(jnp.int32, shape, 0)
    lane = lax.broadcasted_iota(jnp.int32, shape, 1)
    kc = lane % GRID_W
    col_start = jnp.clip(qc - NA_KW // 2, 0, GRID_W - NA_KW)
    col_valid = (kc >= col_start) & (kc < col_start + NA_KW)
    valid = {(True, True): col_valid,
             (True, False): col_valid & (lane < GRID_W),
             (False, True): col_valid & (lane >= GRID_W)}
    for hh in range(rpb_ref.shape[0]):
        tiles = {}

        def pair_tile(dr):
            if dr not in tiles:
                row = jnp.broadcast_to(rpb_ref[hh, dr + NA_DR0:dr + NA_DR0 + 1, :], shape)
                tiles[dr] = pltpu.roll(row, LANES - (NA_KW - 1), 1, stride=1, stride_axis=0)
            return tiles[dr]

        for kind, first_row in enumerate((0, NA_SUB_ROWS, GRID_ROWS - NA_SUB_ROWS)):
            win0 = _na_first_key_row(first_row)
            for a in range(NA_SUB_ROWS):
                qr = first_row + a
                r0 = min(max(qr - NA_KH // 2, 0), GRID_ROWS - NA_KH)
                r = (hh % 2) * NA_SUB_TQ + a * GRID_W
                for kl in range(0, NA_SUB_WIN, 2):
                    row_valid = tuple(r0 <= win0 + kl + d < r0 + NA_KH for d in range(2))
                    if row_valid == (False, False):
                        tile = jnp.full(shape, MASK_VALUE, jnp.float32)
                    else:
                        tile = jnp.where(valid[row_valid], pair_tile(win0 + kl - qr), MASK_VALUE)
                    bias_ref[hh // 2, kind, r:r + GRID_W, kl * GRID_W:(kl + 2) * GRID_W] = tile


def _natten_kernel(q_ref, k_ref, v_ref, g_ref, rpb_ref, o_ref, bias_ref):
    @pl.when(pl.program_id(1) == 0)
    def _():
        _na_build_bias(rpb_ref, bias_ref)

    n_sub = GRID_ROWS // NA_SUB_ROWS
    for pair in range(NA_STEP_PAIRS):
        sl = slice(pair * LANES, (pair + 1) * LANES)
        for sub in range(n_sub):
            kind = 0 if sub == 0 else (2 if sub == n_sub - 1 else 1)
            start = _na_first_key_row(sub * NA_SUB_ROWS) * GRID_W
            rows = slice(sub * NA_SUB_TQ, (sub + 1) * NA_SUB_TQ)
            k = k_ref[0, start:start + NA_SUB_TK, sl]
            v = v_ref[0, start:start + NA_SUB_TK, sl]
            o = _pair_attention(q_ref[0, rows, sl], k, v, bias_ref[pair, kind])
            o_ref[0, rows, sl] = (o * _silu(g_ref[0, rows, sl])).astype(jnp.bfloat16)


def _natten(qkvb, gb, rpb_pairs):
    b, s, _ = qkvb.shape
    n_groups = B_WIDTH // (NA_STEP_PAIRS * LANES)
    width = NA_STEP_PAIRS * LANES
    col = lambda c: (lambda p, bi: (bi, 0, c * n_groups + p))
    return pl.pallas_call(
        _natten_kernel,
        grid=(n_groups, b),
        in_specs=[
            pl.BlockSpec((1, s, width), col(0)),
            pl.BlockSpec((1, s, width), col(1)),
            pl.BlockSpec((1, s, width), col(2)),
            pl.BlockSpec((1, s, width), col(0)),
            pl.BlockSpec((2 * NA_STEP_PAIRS, NA_DR_ROWS, LANES), lambda p, bi: (p, 0, 0)),
        ],
        out_specs=pl.BlockSpec((1, s, width), col(0)),
        out_shape=jax.ShapeDtypeStruct((b, s, B_WIDTH), jnp.bfloat16),
        scratch_shapes=[pltpu.VMEM((NA_STEP_PAIRS, 3, 2 * NA_SUB_TQ, NA_SUB_TK), jnp.float32)],
        compiler_params=pltpu.CompilerParams(
            dimension_semantics=("arbitrary", "arbitrary"), vmem_limit_bytes=VMEM_LIMIT),
        name="natten",
    )(qkvb, qkvb, qkvb, gb, rpb_pairs)


def _na_rpb_pairs(rpb):
    n_dr = 2 * NA_KH - 1
    half = jnp.pad(rpb * LOG2E, ((0, 0), (0, 0), (0, GRID_W - rpb.shape[2])))
    lo_pad = NA_DR0 - (NA_KH - 1)
    ext = jnp.pad(half, ((0, 0), (lo_pad, NA_DR_ROWS + 1 - lo_pad - n_dr), (0, 0)))
    return jnp.concatenate([ext[:, :-1], ext[:, 1:]], axis=-1)


def _rope_tables(seq):
    t = np.arange(seq)
    inv = ROPE_THETA ** (-np.arange(ROPE_PAIRS, dtype=np.float64) * (2.0 / (HEAD_DIM // 2)))
    ang_r = (t // GRID_W)[:, None] * inv[None, :]
    ang_c = (t % GRID_W)[:, None] * inv[None, :]
    ang = np.concatenate([ang_r, ang_r, ang_c, ang_c], axis=-1)
    sign = np.where(np.arange(HEAD_DIM) % (2 * ROPE_PAIRS) < ROPE_PAIRS, -1.0, 1.0)
    reps = (1, LANES // HEAD_DIM)
    cos = np.tile(np.cos(ang), reps).astype(np.float32)
    sin_signed = np.tile(np.sin(ang) * sign, reps).astype(np.float32)
    return jnp.asarray(cos), jnp.asarray(sin_signed)


def kernel(x, norm_gain, w_in, q_norm_a, k_norm_a, na_rpb, w_out, final_norm_gain):
    assert w_in.shape[0] == 1, "single-layer trunk only"
    assert x.shape[1:] == (SEQ, D_MODEL) and w_in.shape[2] == C_END
    cos, sin_signed = _rope_tables(SEQ)
    reps = LANES // HEAD_DIM
    qgain = jnp.tile(q_norm_a[0] * Q_SCALE, reps)[None, :]
    kgain = jnp.tile(k_norm_a[0], reps)[None, :]
    qa, kva, ga, gb, qkvb = _in_proj(x, norm_gain, w_in[0].astype(jnp.bfloat16), qgain, kgain, cos, sin_signed)
    mixed_b = _natten(qkvb, gb, _na_rpb_pairs(na_rpb[0]))
    return _gqa_out(qa, kva, ga, mixed_b, x, w_out[0].astype(jnp.bfloat16), final_norm_gain[None, :])
```

```python
import math

import jax
import jax.numpy as jnp
import numpy as np
from jax import lax
from jax.experimental import pallas as pl
from jax.experimental.pallas import tpu as pltpu

D_MODEL = 1024
SEQ = 2048
GRID_W = 64
GRID_ROWS = SEQ // GRID_W
HEAD_DIM = 64
A_HEADS = 8
A_KV_HEADS = 2
B_HEADS = 8
A_WIDTH = A_HEADS * HEAD_DIM
A_KV_WIDTH = A_KV_HEADS * HEAD_DIM
B_WIDTH = B_HEADS * HEAD_DIM
ROPE_PAIRS = HEAD_DIM // 4
ROPE_THETA = 10000.0
NA_KH = 8
NA_KW = 16
EPS = 1e-6
LOG2E = math.log2(math.e)
Q_SCALE = HEAD_DIM ** -0.5 * LOG2E

LANES = 128
MASK_VALUE = -1e30

IN_TM = 1024
GQA_TQ = 1024
GQA_SUB_TQ = 256
GQA_OUT_ROWS = 1024
NA_SUB_ROWS = 4
NA_SUB_WIN = NA_SUB_ROWS + NA_KH
NA_SUB_TQ = NA_SUB_ROWS * GRID_W
NA_SUB_TK = NA_SUB_WIN * GRID_W
NA_DR0 = 11
NA_DR_ROWS = 24
NA_STEP_PAIRS = 2
VMEM_LIMIT = 48 * 1024 * 1024
GQA_VMEM_LIMIT = 56 * 1024 * 1024

C_QA = 0
C_KA = C_QA + A_WIDTH
C_GA = C_KA + 2 * A_KV_WIDTH
C_QB = C_GA + A_WIDTH
C_KB = C_QB + B_WIDTH
C_GB = C_KB + 2 * B_WIDTH
C_END = C_GB + B_WIDTH


def _lane_lo():
    return lax.broadcasted_iota(jnp.int32, (1, LANES), 1) < HEAD_DIM


def _silu(g):
    return g * (1.0 / (1.0 + jnp.exp(-g)))


def _head_norm_rope(y, gain, cos, sin_signed):
    lo = _lane_lo()
    y2 = y * y
    s_lo = jnp.sum(jnp.where(lo, y2, 0.0), axis=-1, keepdims=True)
    s_hi = jnp.sum(jnp.where(lo, 0.0, y2), axis=-1, keepdims=True)
    ms = jnp.where(lo, s_lo, s_hi) * (1.0 / HEAD_DIM)
    yn = (y * lax.rsqrt(ms + EPS)) * gain
    lane = lax.broadcasted_iota(jnp.int32, (1, LANES), 1)
    first_half = lane % (2 * ROPE_PAIRS) < ROPE_PAIRS
    nxt = pltpu.roll(yn, LANES - ROPE_PAIRS, 1)
    prv = pltpu.roll(yn, ROPE_PAIRS, 1)
    rot = jnp.where(first_half, nxt, prv)
    return yn * cos + rot * sin_signed


def _dup_heads(y):
    lo = _lane_lo()
    swapped = pltpu.roll(y, HEAD_DIM, 1)
    return jnp.where(lo, y, swapped), jnp.where(lo, swapped, y)


def _in_proj_kernel(x_ref, gain_ref, w_ref, qgain_ref, kgain_ref, cos_ref, sin_ref,
                    qa_ref, kva_ref, ga_ref, gb_ref, qkvb_ref):
    x = x_ref[0]
    ms = jnp.mean(x * x, axis=-1, keepdims=True)
    h = ((x * lax.rsqrt(ms + EPS)) * gain_ref[...]).astype(jnp.bfloat16)
    cos = cos_ref[...]
    sin = sin_ref[...]
    bf16 = jnp.bfloat16

    qkva = jnp.dot(h, w_ref[:, C_QA:C_GA], preferred_element_type=jnp.float32)
    for j in range(A_WIDTH // LANES):
        sl = slice(j * LANES, (j + 1) * LANES)
        qa_ref[0, :, sl] = _head_norm_rope(qkva[:, sl], qgain_ref[...], cos, sin).astype(bf16)
    ka = _head_norm_rope(qkva[:, C_KA:C_KA + LANES], kgain_ref[...], cos, sin)
    va = qkva[:, C_KA + LANES:C_GA]
    for n, slab in enumerate(_dup_heads(ka) + _dup_heads(va)):
        kva_ref[0, :, n * LANES:(n + 1) * LANES] = slab.astype(bf16)

    ga_ref[0] = jnp.dot(h, w_ref[:, C_GA:C_QB], preferred_element_type=jnp.float32)
    qb = jnp.dot(h, w_ref[:, C_QB:C_KB], preferred_element_type=jnp.float32)
    qkvb_ref[0, :, :B_WIDTH] = (qb * Q_SCALE).astype(bf16)
    qkvb_ref[0, :, B_WIDTH:] = jnp.dot(h, w_ref[:, C_KB:C_GB], preferred_element_type=jnp.float32).astype(bf16)
    gb_ref[0] = jnp.dot(h, w_ref[:, C_GB:C_END], preferred_element_type=jnp.float32)


def _in_proj(x, gain, w, qgain, kgain, cos, sin_signed):
    b, s, d = x.shape
    row = lambda bi, si: (bi, si, 0)
    const2 = lambda bi, si: (0, 0)
    tab = lambda bi, si: (si, 0)
    widths = (A_WIDTH, 4 * LANES, A_WIDTH, B_WIDTH, 3 * B_WIDTH)
    dtypes = (jnp.bfloat16, jnp.bfloat16, jnp.float32, jnp.float32, jnp.bfloat16)
    return pl.pallas_call(
        _in_proj_kernel,
        grid=(b, s // IN_TM),
        in_specs=[
            pl.BlockSpec((1, IN_TM, d), row),
            pl.BlockSpec((1, d), const2),
            pl.BlockSpec((d, C_END), const2),
            pl.BlockSpec((1, LANES), const2),
            pl.BlockSpec((1, LANES), const2),
            pl.BlockSpec((IN_TM, LANES), tab),
            pl.BlockSpec((IN_TM, LANES), tab),
        ],
        out_specs=[pl.BlockSpec((1, IN_TM, n), row) for n in widths],
        out_shape=[jax.ShapeDtypeStruct((b, s, n), t) for n, t in zip(widths, dtypes)],
        compiler_params=pltpu.CompilerParams(
            dimension_semantics=("arbitrary", "arbitrary"), vmem_limit_bytes=VMEM_LIMIT),
        name="in_proj",
    )(x, gain, w, qgain, kgain, cos, sin_signed)


def _pair_attention(q, k, v, bias):
    lo = _lane_lo()
    tq = q.shape[0]
    one = jnp.ones((), v.dtype)
    zero = jnp.zeros((), q.dtype)
    qs = jnp.concatenate([jnp.where(lo, q, zero), jnp.where(lo, zero, q)], axis=0)
    vs = jnp.concatenate([jnp.where(lo, v, one), jnp.where(lo, one, v)], axis=1)
    s = lax.dot_general(qs, k, (((1,), (1,)), ((), ())), preferred_element_type=jnp.float32)
    if bias is not None:
        s = s + bias
    m = jnp.max(s, axis=-1, keepdims=True)
    p = jnp.exp2(s - m).astype(jnp.bfloat16)
    o2 = jnp.dot(p, vs, preferred_element_type=jnp.float32)
    a = o2[:tq, :LANES]
    b = o2[tq:, LANES:]
    numer = jnp.where(lo, a, b)
    denom = jnp.where(lo, pltpu.roll(a, HEAD_DIM, 1), pltpu.roll(b, HEAD_DIM, 1))
    return numer / denom


def _gqa_out_kernel(qa_ref, kva_ref, g_ref, mb_ref, x_ref, w_ref, gain_ref, o_ref, ma_ref):
    pairs_per_kv = (A_HEADS // A_KV_HEADS) // 2
    subs_per_out = GQA_OUT_ROWS // GQA_SUB_TQ
    for sub in range(GQA_TQ // GQA_SUB_TQ):
        rows = slice(sub * GQA_SUB_TQ, (sub + 1) * GQA_SUB_TQ)
        for j in range(A_WIDTH // LANES):
            kv = j // pairs_per_kv
            k = kva_ref[0, :, kv * LANES:(kv + 1) * LANES]
            v = kva_ref[0, :, (A_KV_HEADS + kv) * LANES:(A_KV_HEADS + kv + 1) * LANES]
            sl = slice(j * LANES, (j + 1) * LANES)
            o = _pair_attention(qa_ref[0, rows, sl], k, v, None)
            ma_ref[rows, sl] = (o * _silu(g_ref[0, rows, sl])).astype(jnp.bfloat16)
        if (sub + 1) % subs_per_out == 0:
            rows = slice((sub + 1) * GQA_SUB_TQ - GQA_OUT_ROWS, (sub + 1) * GQA_SUB_TQ)
            y = x_ref[0, rows, :]
            y = y + jnp.dot(ma_ref[rows, :], w_ref[:A_WIDTH, :], preferred_element_type=jnp.float32)
            y = y + jnp.dot(mb_ref[0, rows, :], w_ref[A_WIDTH:, :], preferred_element_type=jnp.float32)
            ms = jnp.mean(y * y, axis=-1, keepdims=True)
            o_ref[0, rows, :] = (y * lax.rsqrt(ms + EPS)) * gain_ref[...]


def _gqa_out(qa, kva, ga, mb, x, w, gain):
    b, s, d = x.shape
    tile = lambda bi, qi: (bi, qi, 0)
    const2 = lambda bi, qi: (0, 0)
    return pl.pallas_call(
        _gqa_out_kernel,
        grid=(b, s // GQA_TQ),
        in_specs=[
            pl.BlockSpec((1, GQA_TQ, A_WIDTH), tile),
            pl.BlockSpec((1, s, kva.shape[2]), lambda bi, qi: (bi, 0, 0)),
            pl.BlockSpec((1, GQA_TQ, A_WIDTH), tile),
            pl.BlockSpec((1, GQA_TQ, B_WIDTH), tile),
            pl.BlockSpec((1, GQA_TQ, d), tile),
            pl.BlockSpec((A_WIDTH + B_WIDTH, d), const2),
            pl.BlockSpec((1, d), const2),
        ],
        out_specs=pl.BlockSpec((1, GQA_TQ, d), tile),
        out_shape=jax.ShapeDtypeStruct((b, s, d), jnp.float32),
        scratch_shapes=[pltpu.VMEM((GQA_TQ, A_WIDTH), jnp.bfloat16)],
        compiler_params=pltpu.CompilerParams(
            dimension_semantics=("arbitrary", "arbitrary"), vmem_limit_bytes=GQA_VMEM_LIMIT),
        name="gqa_out",
    )(qa, kva, ga, mb, x, w, gain)


def _na_first_key_row(first_query_row):
    return min(max(first_query_row - NA_KH // 2, 0), GRID_ROWS - NA_SUB_WIN)


def _na_build_bias(rpb_ref, bias_ref):
    shape = (GRID_W, LANES)
    qc = lax.broadcasted_iota(jnp.int32, shape, 0)
    lane = lax.broadcasted_iota(jnp.int32, shape, 1)
    kc = lane % GRID_W
    col_start = jnp.clip(qc - NA_KW // 2, 0, GRID_W - NA_KW)
    col_valid = (kc >= col_start) & (kc < col_start + NA_KW)
    valid = {(True, True): col_valid,
             (True, False): col_valid & (lane < GRID_W),
             (False, True): col_valid & (lane >= GRID_W)}
    for hh in range(rpb_ref.shape[0]):
        tiles = {}

        def pair_tile(dr):
            if dr not in tiles:
                row = jnp.broadcast_to(rpb_ref[hh, dr + NA_DR0:dr + NA_DR0 + 1, :], shape)
                tiles[dr] = pltpu.roll(row, LANES - (NA_KW - 1), 1, stride=1, stride_axis=0)
            return tiles[dr]

        for kind, first_row in enumerate((0, NA_SUB_ROWS, GRID_ROWS - NA_SUB_ROWS)):
            win0 = _na_first_key_row(first_row)
            for a in range(NA_SUB_ROWS):
                qr = first_row + a
                r0 = min(max(qr - NA_KH // 2, 0), GRID_ROWS - NA_KH)
                r = (hh % 2) * NA_SUB_TQ + a * GRID_W
                for kl in range(0, NA_SUB_WIN, 2):
                    row_valid = tuple(r0 <= win0 + kl + d < r0 + NA_KH for d in range(2))
                    if row_valid == (False, False):
                        tile = jnp.full(shape, MASK_VALUE, jnp.float32)
                    else:
                        tile = jnp.where(valid[row_valid], pair_tile(win0 + kl - qr), MASK_VALUE)
                    bias_ref[hh // 2, kind, r:r + GRID_W, kl * GRID_W:(kl + 2) * GRID_W] = tile


def _na_sub_block(sub):
    win0 = _na_first_key_row(sub * NA_SUB_ROWS)
    if sub == 0:
        return 0, win0, NA_KH, 0
    if sub == GRID_ROWS // NA_SUB_ROWS - 1:
        return 2, win0 + NA_SUB_WIN - NA_KH, NA_KH, NA_SUB_WIN - NA_KH
    return 1, win0, NA_SUB_WIN, 0


def _natten_kernel(q_ref, k_ref, v_ref, g_ref, rpb_ref, o_ref, bias_ref):
    @pl.when(pl.program_id(1) == 0)
    def _():
        _na_build_bias(rpb_ref, bias_ref)

    n_sub = GRID_ROWS // NA_SUB_ROWS
    for pair in range(NA_STEP_PAIRS):
        sl = slice(pair * LANES, (pair + 1) * LANES)
        for sub in range(n_sub):
            kind, key0, n_keys, col0 = _na_sub_block(sub)
            keys = slice(key0 * GRID_W, (key0 + n_keys) * GRID_W)
            cols = slice(col0 * GRID_W, (col0 + n_keys) * GRID_W)
            rows = slice(sub * NA_SUB_TQ, (sub + 1) * NA_SUB_TQ)
            o = _pair_attention(q_ref[0, rows, sl], k_ref[0, keys, sl], v_ref[0, keys, sl],
                                bias_ref[pair, kind, :, cols])
            o_ref[0, rows, sl] = (o * _silu(g_ref[0, rows, sl])).astype(jnp.bfloat16)


def _natten(qkvb, gb, rpb_pairs):
    b, s, _ = qkvb.shape
    n_groups = B_WIDTH // (NA_STEP_PAIRS * LANES)
    width = NA_STEP_PAIRS * LANES
    col = lambda c: (lambda p, bi: (bi, 0, c * n_groups + p))
    return pl.pallas_call(
        _natten_kernel,
        grid=(n_groups, b),
        in_specs=[
            pl.BlockSpec((1, s, width), col(0)),
            pl.BlockSpec((1, s, width), col(1)),
            pl.BlockSpec((1, s, width), col(2)),
            pl.BlockSpec((1, s, width), col(0)),
            pl.BlockSpec((2 * NA_STEP_PAIRS, NA_DR_ROWS, LANES), lambda p, bi: (p, 0, 0)),
        ],
        out_specs=pl.BlockSpec((1, s, width), col(0)),
        out_shape=jax.ShapeDtypeStruct((b, s, B_WIDTH), jnp.bfloat16),
        scratch_shapes=[pltpu.VMEM((NA_STEP_PAIRS, 3, 2 * NA_SUB_TQ, NA_SUB_TK), jnp.float32)],
        compiler_params=pltpu.CompilerParams(
            dimension_semantics=("arbitrary", "arbitrary"), vmem_limit_bytes=VMEM_LIMIT),
        name="natten",
    )(qkvb, qkvb, qkvb, gb, rpb_pairs)


def _na_rpb_pairs(rpb):
    n_dr = 2 * NA_KH - 1
    half = jnp.pad(rpb * LOG2E, ((0, 0), (0, 0), (0, GRID_W - rpb.shape[2])))
    lo_pad = NA_DR0 - (NA_KH - 1)
    ext = jnp.pad(half, ((0, 0), (lo_pad, NA_DR_ROWS + 1 - lo_pad - n_dr), (0, 0)))
    return jnp.concatenate([ext[:, :-1], ext[:, 1:]], axis=-1)


def _rope_tables(seq):
    t = np.arange(seq)
    inv = ROPE_THETA ** (-np.arange(ROPE_PAIRS, dtype=np.float64) * (2.0 / (HEAD_DIM // 2)))
    ang_r = (t // GRID_W)[:, None] * inv[None, :]
    ang_c = (t % GRID_W)[:, None] * inv[None, :]
    ang = np.concatenate([ang_r, ang_r, ang_c, ang_c], axis=-1)
    sign = np.where(np.arange(HEAD_DIM) % (2 * ROPE_PAIRS) < ROPE_PAIRS, -1.0, 1.0)
    reps = (1, LANES // HEAD_DIM)
    cos = np.tile(np.cos(ang), reps).astype(np.float32)
    sin_signed = np.tile(np.sin(ang) * sign, reps).astype(np.float32)
    return jnp.asarray(cos), jnp.asarray(sin_signed)


def kernel(x, norm_gain, w_in, q_norm_a, k_norm_a, na_rpb, w_out, final_norm_gain):
    assert w_in.shape[0] == 1, "single-layer trunk only"
    assert x.shape[1:] == (SEQ, D_MODEL) and w_in.shape[2] == C_END
    cos, sin_signed = _rope_tables(SEQ)
    reps = LANES // HEAD_DIM
    qgain = jnp.tile(q_norm_a[0] * Q_SCALE, reps)[None, :]
    kgain = jnp.tile(k_norm_a[0], reps)[None, :]
    qa, kva, ga, gb, qkvb = _in_proj(x, norm_gain, w_in[0].astype(jnp.bfloat16), qgain, kgain, cos, sin_signed)
    mixed_b = _natten(qkvb, gb, _na_rpb_pairs(na_rpb[0]))
    return _gqa_out(qa, kva, ga, mixed_b, x, w_out[0].astype(jnp.bfloat16), final_norm_gain[None, :])
```

```python
import math

import jax
import jax.numpy as jnp
import numpy as np
from jax import lax
from jax.experimental import pallas as pl
from jax.experimental.pallas import tpu as pltpu

D_MODEL = 1024
SEQ = 2048
GRID_W = 64
GRID_ROWS = SEQ // GRID_W
HEAD_DIM = 64
A_HEADS = 8
A_KV_HEADS = 2
B_HEADS = 8
A_WIDTH = A_HEADS * HEAD_DIM
A_KV_WIDTH = A_KV_HEADS * HEAD_DIM
B_WIDTH = B_HEADS * HEAD_DIM
ROPE_PAIRS = HEAD_DIM // 4
ROPE_THETA = 10000.0
NA_KH = 8
NA_KW = 16
EPS = 1e-6
LOG2E = math.log2(math.e)
Q_SCALE = HEAD_DIM ** -0.5 * LOG2E

LANES = 128
MASK_VALUE = -1e30

IN_TM = 1024
GQA_TQ = 1024
GQA_SUB_TQ = 256
GQA_OUT_ROWS = 1024
NA_SUB_ROWS = 4
NA_SUB_WIN = NA_SUB_ROWS + NA_KH
NA_SUB_TQ = NA_SUB_ROWS * GRID_W
NA_SUB_TK = NA_SUB_WIN * GRID_W
NA_DR0 = 11
NA_DR_ROWS = 24
NA_STEP_PAIRS = 2
VMEM_LIMIT = 48 * 1024 * 1024
GQA_VMEM_LIMIT = 56 * 1024 * 1024

C_QA = 0
C_KA = C_QA + A_WIDTH
C_GA = C_KA + 2 * A_KV_WIDTH
C_QB = C_GA + A_WIDTH
C_KB = C_QB + B_WIDTH
C_GB = C_KB + 2 * B_WIDTH
C_END = C_GB + B_WIDTH


def _lane_lo():
    return lax.broadcasted_iota(jnp.int32, (1, LANES), 1) < HEAD_DIM


def _silu(g):
    return g * (1.0 / (1.0 + jnp.exp(-g)))


def _head_norm_rope(y, gain, cos, sin_signed):
    lo = _lane_lo()
    y2 = y * y
    s_lo = jnp.sum(jnp.where(lo, y2, 0.0), axis=-1, keepdims=True)
    s_hi = jnp.sum(jnp.where(lo, 0.0, y2), axis=-1, keepdims=True)
    ms = jnp.where(lo, s_lo, s_hi) * (1.0 / HEAD_DIM)
    yn = (y * lax.rsqrt(ms + EPS)) * gain
    lane = lax.broadcasted_iota(jnp.int32, (1, LANES), 1)
    first_half = lane % (2 * ROPE_PAIRS) < ROPE_PAIRS
    nxt = pltpu.roll(yn, LANES - ROPE_PAIRS, 1)
    prv = pltpu.roll(yn, ROPE_PAIRS, 1)
    rot = jnp.where(first_half, nxt, prv)
    return yn * cos + rot * sin_signed


def _dup_heads(y):
    lo = _lane_lo()
    swapped = pltpu.roll(y, HEAD_DIM, 1)
    return jnp.where(lo, y, swapped), jnp.where(lo, swapped, y)


def _in_proj_kernel(x_ref, gain_ref, w_ref, qgain_ref, kgain_ref, cos_ref, sin_ref,
                    qa_ref, kva_ref, ga_ref, gb_ref, qkvb_ref):
    x = x_ref[0]
    ms = jnp.mean(x * x, axis=-1, keepdims=True)
    h = ((x * lax.rsqrt(ms + EPS)) * gain_ref[...]).astype(jnp.bfloat16)
    cos = cos_ref[...]
    sin = sin_ref[...]
    bf16 = jnp.bfloat16

    qkva = jnp.dot(h, w_ref[:, C_QA:C_GA], preferred_element_type=jnp.float32)
    for j in range(A_WIDTH // LANES):
        sl = slice(j * LANES, (j + 1) * LANES)
        qa_ref[0, :, sl] = _head_norm_rope(qkva[:, sl], qgain_ref[...], cos, sin).astype(bf16)
    ka = _head_norm_rope(qkva[:, C_KA:C_KA + LANES], kgain_ref[...], cos, sin)
    va = qkva[:, C_KA + LANES:C_GA]
    for n, slab in enumerate(_dup_heads(ka) + _dup_heads(va)):
        kva_ref[0, :, n * LANES:(n + 1) * LANES] = slab.astype(bf16)

    ga_ref[0] = jnp.dot(h, w_ref[:, C_GA:C_QB], preferred_element_type=jnp.float32)
    qb = jnp.dot(h, w_ref[:, C_QB:C_KB], preferred_element_type=jnp.float32)
    qkvb_ref[0, :, :B_WIDTH] = (qb * Q_SCALE).astype(bf16)
    qkvb_ref[0, :, B_WIDTH:] = jnp.dot(h, w_ref[:, C_KB:C_GB], preferred_element_type=jnp.float32).astype(bf16)
    gb_ref[0] = jnp.dot(h, w_ref[:, C_GB:C_END], preferred_element_type=jnp.float32)


def _in_proj(x, gain, w, qgain, kgain, cos, sin_signed):
    b, s, d = x.shape
    row = lambda bi, si: (bi, si, 0)
    const2 = lambda bi, si: (0, 0)
    tab = lambda bi, si: (si, 0)
    widths = (A_WIDTH, 4 * LANES, A_WIDTH, B_WIDTH, 3 * B_WIDTH)
    dtypes = (jnp.bfloat16, jnp.bfloat16, jnp.float32, jnp.float32, jnp.bfloat16)
    return pl.pallas_call(
        _in_proj_kernel,
        grid=(b, s // IN_TM),
        in_specs=[
            pl.BlockSpec((1, IN_TM, d), row),
            pl.BlockSpec((1, d), const2),
            pl.BlockSpec((d, C_END), const2),
            pl.BlockSpec((1, LANES), const2),
            pl.BlockSpec((1, LANES), const2),
            pl.BlockSpec((IN_TM, LANES), tab),
            pl.BlockSpec((IN_TM, LANES), tab),
        ],
        out_specs=[pl.BlockSpec((1, IN_TM, n), row) for n in widths],
        out_shape=[jax.ShapeDtypeStruct((b, s, n), t) for n, t in zip(widths, dtypes)],
        compiler_params=pltpu.CompilerParams(
            dimension_semantics=("arbitrary", "arbitrary"), vmem_limit_bytes=VMEM_LIMIT),
        name="in_proj",
    )(x, gain, w, qgain, kgain, cos, sin_signed)


def _pair_attention(q, k, v, bias, live_cols=None):
    lo = _lane_lo()
    tq = q.shape[0]
    one = jnp.ones((), v.dtype)
    zero = jnp.zeros((), q.dtype)
    qs = jnp.concatenate([jnp.where(lo, q, zero), jnp.where(lo, zero, q)], axis=0)
    vs = jnp.concatenate([jnp.where(lo, v, one), jnp.where(lo, one, v)], axis=1)
    s = lax.dot_general(qs, k, (((1,), (1,)), ((), ())), preferred_element_type=jnp.float32)
    if bias is None:
        m = jnp.max(s, axis=-1, keepdims=True)
        p = jnp.exp2(s - m).astype(jnp.bfloat16)
    else:
        tk = k.shape[0]
        slabs = []
        for n, (c0, c1) in enumerate(live_cols):
            r = slice(n * GRID_W, (n + 1) * GRID_W)
            sb = s[r, c0:c1] + bias[r, c0:c1]
            pb = jnp.exp2(sb - jnp.max(sb, axis=-1, keepdims=True)).astype(jnp.bfloat16)
            parts = [jnp.zeros((GRID_W, c0), pb.dtype)] * (c0 > 0) + [pb]
            parts += [jnp.zeros((GRID_W, tk - c1), pb.dtype)] * (c1 < tk)
            slabs.append(jnp.concatenate(parts, axis=1) if len(parts) > 1 else pb)
        p = jnp.concatenate(slabs, axis=0)
    o2 = jnp.dot(p, vs, preferred_element_type=jnp.float32)
    a = o2[:tq, :LANES]
    b = o2[tq:, LANES:]
    numer = jnp.where(lo, a, b)
    denom = jnp.where(lo, pltpu.roll(a, HEAD_DIM, 1), pltpu.roll(b, HEAD_DIM, 1))
    return numer / denom


def _gqa_out_kernel(qa_ref, kva_ref, g_ref, mb_ref, x_ref, w_ref, gain_ref, o_ref, ma_ref):
    pairs_per_kv = (A_HEADS // A_KV_HEADS) // 2
    subs_per_out = GQA_OUT_ROWS // GQA_SUB_TQ
    for sub in range(GQA_TQ // GQA_SUB_TQ):
        rows = slice(sub * GQA_SUB_TQ, (sub + 1) * GQA_SUB_TQ)
        for j in range(A_WIDTH // LANES):
            kv = j // pairs_per_kv
            k = kva_ref[0, :, kv * LANES:(kv + 1) * LANES]
            v = kva_ref[0, :, (A_KV_HEADS + kv) * LANES:(A_KV_HEADS + kv + 1) * LANES]
            sl = slice(j * LANES, (j + 1) * LANES)
            o = _pair_attention(qa_ref[0, rows, sl], k, v, None)
            ma_ref[rows, sl] = (o * _silu(g_ref[0, rows, sl])).astype(jnp.bfloat16)
        if (sub + 1) % subs_per_out == 0:
            rows = slice((sub + 1) * GQA_SUB_TQ - GQA_OUT_ROWS, (sub + 1) * GQA_SUB_TQ)
            y = x_ref[0, rows, :]
            y = y + jnp.dot(ma_ref[rows, :], w_ref[:A_WIDTH, :], preferred_element_type=jnp.float32)
            y = y + jnp.dot(mb_ref[0, rows, :], w_ref[A_WIDTH:, :], preferred_element_type=jnp.float32)
            ms = jnp.mean(y * y, axis=-1, keepdims=True)
            o_ref[0, rows, :] = (y * lax.rsqrt(ms + EPS)) * gain_ref[...]


def _gqa_out(qa, kva, ga, mb, x, w, gain):
    b, s, d = x.shape
    tile = lambda bi, qi: (bi, qi, 0)
    const2 = lambda bi, qi: (0, 0)
    return pl.pallas_call(
        _gqa_out_kernel,
        grid=(b, s // GQA_TQ),
        in_specs=[
            pl.BlockSpec((1, GQA_TQ, A_WIDTH), tile),
            pl.BlockSpec((1, s, kva.shape[2]), lambda bi, qi: (bi, 0, 0)),
            pl.BlockSpec((1, GQA_TQ, A_WIDTH), tile),
            pl.BlockSpec((1, GQA_TQ, B_WIDTH), tile),
            pl.BlockSpec((1, GQA_TQ, d), tile),
            pl.BlockSpec((A_WIDTH + B_WIDTH, d), const2),
            pl.BlockSpec((1, d), const2),
        ],
        out_specs=pl.BlockSpec((1, GQA_TQ, d), tile),
        out_shape=jax.ShapeDtypeStruct((b, s, d), jnp.float32),
        scratch_shapes=[pltpu.VMEM((GQA_TQ, A_WIDTH), jnp.bfloat16)],
        compiler_params=pltpu.CompilerParams(
            dimension_semantics=("arbitrary", "arbitrary"), vmem_limit_bytes=GQA_VMEM_LIMIT),
        name="gqa_out",
    )(qa, kva, ga, mb, x, w, gain)


def _na_first_key_row(first_query_row):
    return min(max(first_query_row - NA_KH // 2, 0), GRID_ROWS - NA_SUB_WIN)


def _na_build_bias(rpb_ref, bias_ref):
    shape = (GRID_W, LANES)
    qc = lax.broadcasted_iota(jnp.int32, shape, 0)
    lane = lax.broadcasted_iota(jnp.int32, shape, 1)
    kc = lane % GRID_W
    col_start = jnp.clip(qc - NA_KW // 2, 0, GRID_W - NA_KW)
    col_valid = (kc >= col_start) & (kc < col_start + NA_KW)
    valid = {(True, True): col_valid,
             (True, False): col_valid & (lane < GRID_W),
             (False, True): col_valid & (lane >= GRID_W)}
    for hh in range(rpb_ref.shape[0]):
        tiles = {}

        def pair_tile(dr):
            if dr not in tiles:
                row = jnp.broadcast_to(rpb_ref[hh, dr + NA_DR0:dr + NA_DR0 + 1, :], shape)
                tiles[dr] = pltpu.roll(row, LANES - (NA_KW - 1), 1, stride=1, stride_axis=0)
            return tiles[dr]

        for kind, first_row in enumerate((0, NA_SUB_ROWS, GRID_ROWS - NA_SUB_ROWS)):
            win0 = _na_first_key_row(first_row)
            for a in range(NA_SUB_ROWS):
                qr = first_row + a
                r0 = min(max(qr - NA_KH // 2, 0), GRID_ROWS - NA_KH)
                r = (hh % 2) * NA_SUB_TQ + a * GRID_W
                for kl in range(0, NA_SUB_WIN, 2):
                    row_valid = tuple(r0 <= win0 + kl + d < r0 + NA_KH for d in range(2))
                    if row_valid == (False, False):
                        tile = jnp.full(shape, MASK_VALUE, jnp.float32)
                    else:
                        tile = jnp.where(valid[row_valid], pair_tile(win0 + kl - qr), MASK_VALUE)
                    bias_ref[hh // 2, kind, r:r + GRID_W, kl * GRID_W:(kl + 2) * GRID_W] = tile


def _na_sub_block(sub):
    win0 = _na_first_key_row(sub * NA_SUB_ROWS)
    if sub == 0:
        return 0, win0, NA_KH, 0
    if sub == GRID_ROWS // NA_SUB_ROWS - 1:
        return 2, win0 + NA_SUB_WIN - NA_KH, NA_KH, NA_SUB_WIN - NA_KH
    return 1, win0, NA_SUB_WIN, 0


def _na_live_cols(sub, key0, n_keys):
    live = []
    for a in range(NA_SUB_ROWS):
        r0 = min(max(sub * NA_SUB_ROWS + a - NA_KH // 2, 0), GRID_ROWS - NA_KH)
        lo, hi = r0 - key0, r0 - key0 + NA_KH
        live.append(((lo // 2) * LANES, min(-(-hi // 2), n_keys // 2) * LANES))
    return live * 2


def _natten_kernel(q_ref, k_ref, v_ref, g_ref, rpb_ref, o_ref, bias_ref):
    @pl.when(pl.program_id(1) == 0)
    def _():
        _na_build_bias(rpb_ref, bias_ref)

    n_sub = GRID_ROWS // NA_SUB_ROWS
    for pair in range(NA_STEP_PAIRS):
        sl = slice(pair * LANES, (pair + 1) * LANES)
        for sub in range(n_sub):
            kind, key0, n_keys, col0 = _na_sub_block(sub)
            keys = slice(key0 * GRID_W, (key0 + n_keys) * GRID_W)
            cols = slice(col0 * GRID_W, (col0 + n_keys) * GRID_W)
            rows = slice(sub * NA_SUB_TQ, (sub + 1) * NA_SUB_TQ)
            o = _pair_attention(q_ref[0, rows, sl], k_ref[0, keys, sl], v_ref[0, keys, sl],
                                bias_ref[pair, kind, :, cols], _na_live_cols(sub, key0, n_keys))
            o_ref[0, rows, sl] = (o * _silu(g_ref[0, rows, sl])).astype(jnp.bfloat16)


def _natten(qkvb, gb, rpb_pairs):
    b, s, _ = qkvb.shape
    n_groups = B_WIDTH // (NA_STEP_PAIRS * LANES)
    width = NA_STEP_PAIRS * LANES
    col = lambda c: (lambda p, bi: (bi, 0, c * n_groups + p))
    return pl.pallas_call(
        _natten_kernel,
        grid=(n_groups, b),
        in_specs=[
            pl.BlockSpec((1, s, width), col(0)),
            pl.BlockSpec((1, s, width), col(1)),
            pl.BlockSpec((1, s, width), col(2)),
            pl.BlockSpec((1, s, width), col(0)),
            pl.BlockSpec((2 * NA_STEP_PAIRS, NA_DR_ROWS, LANES), lambda p, bi: (p, 0, 0)),
        ],
        out_specs=pl.BlockSpec((1, s, width), col(0)),
        out_shape=jax.ShapeDtypeStruct((b, s, B_WIDTH), jnp.bfloat16),
        scratch_shapes=[pltpu.VMEM((NA_STEP_PAIRS, 3, 2 * NA_SUB_TQ, NA_SUB_TK), jnp.float32)],
        compiler_params=pltpu.CompilerParams(
            dimension_semantics=("arbitrary", "arbitrary"), vmem_limit_bytes=VMEM_LIMIT),
        name="natten",
    )(qkvb, qkvb, qkvb, gb, rpb_pairs)


def _na_rpb_pairs(rpb):
    n_dr = 2 * NA_KH - 1
    half = jnp.pad(rpb * LOG2E, ((0, 0), (0, 0), (0, GRID_W - rpb.shape[2])))
    lo_pad = NA_DR0 - (NA_KH - 1)
    ext = jnp.pad(half, ((0, 0), (lo_pad, NA_DR_ROWS + 1 - lo_pad - n_dr), (0, 0)))
    return jnp.concatenate([ext[:, :-1], ext[:, 1:]], axis=-1)


def _rope_tables(seq):
    t = np.arange(seq)
    inv = ROPE_THETA ** (-np.arange(ROPE_PAIRS, dtype=np.float64) * (2.0 / (HEAD_DIM // 2)))
    ang_r = (t // GRID_W)[:, None] * inv[None, :]
    ang_c = (t % GRID_W)[:, None] * inv[None, :]
    ang = np.concatenate([ang_r, ang_r, ang_c, ang_c], axis=-1)
    sign = np.where(np.arange(HEAD_DIM) % (2 * ROPE_PAIRS) < ROPE_PAIRS, -1.0, 1.0)
    reps = (1, LANES // HEAD_DIM)
    cos = np.tile(np.cos(ang), reps).astype(np.float32)
    sin_signed = np.tile(np.sin(ang) * sign, reps).astype(np.float32)
    return jnp.asarray(cos), jnp.asarray(sin_signed)


def kernel(x, norm_gain, w_in, q_norm_a, k_norm_a, na_rpb, w_out, final_norm_gain):
    assert w_in.shape[0] == 1, "single-layer trunk only"
    assert x.shape[1:] == (SEQ, D_MODEL) and w_in.shape[2] == C_END
    cos, sin_signed = _rope_tables(SEQ)
    reps = LANES // HEAD_DIM
    qgain = jnp.tile(q_norm_a[0] * Q_SCALE, reps)[None, :]
    kgain = jnp.tile(k_norm_a[0], reps)[None, :]
    qa, kva, ga, gb, qkvb = _in_proj(x, norm_gain, w_in[0].astype(jnp.bfloat16), qgain, kgain, cos, sin_signed)
    mixed_b = _natten(qkvb, gb, _na_rpb_pairs(na_rpb[0]))
    return _gqa_out(qa, kva, ga, mixed_b, x, w_out[0].astype(jnp.bfloat16), final_norm_gain[None, :])
```

```python
import math

import jax
import jax.numpy as jnp
import numpy as np
from jax import lax
from jax.experimental import pallas as pl
from jax.experimental.pallas import tpu as pltpu

D_MODEL = 1024
SEQ = 2048
GRID_W = 64
GRID_ROWS = SEQ // GRID_W
HEAD_DIM = 64
A_HEADS = 8
A_KV_HEADS = 2
B_HEADS = 8
A_WIDTH = A_HEADS * HEAD_DIM
A_KV_WIDTH = A_KV_HEADS * HEAD_DIM
B_WIDTH = B_HEADS * HEAD_DIM
ROPE_PAIRS = HEAD_DIM // 4
ROPE_THETA = 10000.0
NA_KH = 8
NA_KW = 16
EPS = 1e-6
LOG2E = math.log2(math.e)
Q_SCALE = HEAD_DIM ** -0.5 * LOG2E

LANES = 128
MASK_VALUE = -1e30

IN_TM = 1024
GQA_TQ = 1024
GQA_SUB_TQ = 256
GQA_OUT_ROWS = 1024
NA_SUB_ROWS = 4
NA_SUB_WIN = NA_SUB_ROWS + NA_KH
NA_SUB_TQ = NA_SUB_ROWS * GRID_W
NA_SUB_TK = NA_SUB_WIN * GRID_W
NA_DR0 = 11
NA_DR_ROWS = 24
NA_STEP_PAIRS = 2
VMEM_LIMIT = 48 * 1024 * 1024
GQA_VMEM_LIMIT = 56 * 1024 * 1024

C_QA = 0
C_KA = C_QA + A_WIDTH
C_GA = C_KA + 2 * A_KV_WIDTH
C_QB = C_GA + A_WIDTH
C_KB = C_QB + B_WIDTH
C_GB = C_KB + 2 * B_WIDTH
C_END = C_GB + B_WIDTH


def _lane_lo():
    return lax.broadcasted_iota(jnp.int32, (1, LANES), 1) < HEAD_DIM


def _silu(g):
    return g * (1.0 / (1.0 + jnp.exp(-g)))


def _head_norm_rope(y, gain, cos, sin_signed):
    lo = _lane_lo()
    y2 = y * y
    s_lo = jnp.sum(jnp.where(lo, y2, 0.0), axis=-1, keepdims=True)
    s_hi = jnp.sum(jnp.where(lo, 0.0, y2), axis=-1, keepdims=True)
    ms = jnp.where(lo, s_lo, s_hi) * (1.0 / HEAD_DIM)
    yn = (y * lax.rsqrt(ms + EPS)) * gain
    lane = lax.broadcasted_iota(jnp.int32, (1, LANES), 1)
    first_half = lane % (2 * ROPE_PAIRS) < ROPE_PAIRS
    nxt = pltpu.roll(yn, LANES - ROPE_PAIRS, 1)
    prv = pltpu.roll(yn, ROPE_PAIRS, 1)
    rot = jnp.where(first_half, nxt, prv)
    return yn * cos + rot * sin_signed


def _dup_heads(y):
    lo = _lane_lo()
    swapped = pltpu.roll(y, HEAD_DIM, 1)
    return jnp.where(lo, y, swapped), jnp.where(lo, swapped, y)


def _in_proj_kernel(x_ref, gain_ref, w32_ref, qgain_ref, kgain_ref, cos_ref, sin_ref,
                    qa_ref, kva_ref, ga_ref, gb_ref, qkvb_ref, w_ref):
    @pl.when((pl.program_id(0) == 0) & (pl.program_id(1) == 0))
    def _():
        for c in range(0, C_END, 2 * LANES):
            w_ref[:, c:c + 2 * LANES] = w32_ref[:, c:c + 2 * LANES].astype(jnp.bfloat16)

    x = x_ref[0]
    ms = jnp.mean(x * x, axis=-1, keepdims=True)
    h = ((x * lax.rsqrt(ms + EPS)) * gain_ref[...]).astype(jnp.bfloat16)
    cos = cos_ref[...]
    sin = sin_ref[...]
    bf16 = jnp.bfloat16

    qkva = jnp.dot(h, w_ref[:, C_QA:C_GA], preferred_element_type=jnp.float32)
    for j in range(A_WIDTH // LANES):
        sl = slice(j * LANES, (j + 1) * LANES)
        qa_ref[0, :, sl] = _head_norm_rope(qkva[:, sl], qgain_ref[...], cos, sin).astype(bf16)
    ka = _head_norm_rope(qkva[:, C_KA:C_KA + LANES], kgain_ref[...], cos, sin)
    va = qkva[:, C_KA + LANES:C_GA]
    for n, slab in enumerate(_dup_heads(ka) + _dup_heads(va)):
        kva_ref[0, :, n * LANES:(n + 1) * LANES] = slab.astype(bf16)

    ga_ref[0] = jnp.dot(h, w_ref[:, C_GA:C_QB], preferred_element_type=jnp.float32)
    qb = jnp.dot(h, w_ref[:, C_QB:C_KB], preferred_element_type=jnp.float32)
    qkvb_ref[0, :, :B_WIDTH] = (qb * Q_SCALE).astype(bf16)
    qkvb_ref[0, :, B_WIDTH:] = jnp.dot(h, w_ref[:, C_KB:C_GB], preferred_element_type=jnp.float32).astype(bf16)
    gb_ref[0] = jnp.dot(h, w_ref[:, C_GB:C_END], preferred_element_type=jnp.float32)


def _in_proj(x, gain, w, qgain, kgain, cos, sin_signed):
    b, s, d = x.shape
    row = lambda bi, si: (bi, si, 0)
    const2 = lambda bi, si: (0, 0)
    tab = lambda bi, si: (si, 0)
    widths = (A_WIDTH, 4 * LANES, A_WIDTH, B_WIDTH, 3 * B_WIDTH)
    dtypes = (jnp.bfloat16, jnp.bfloat16, jnp.float32, jnp.float32, jnp.bfloat16)
    return pl.pallas_call(
        _in_proj_kernel,
        grid=(b, s // IN_TM),
        in_specs=[
            pl.BlockSpec((1, IN_TM, d), row),
            pl.BlockSpec((1, d), const2),
            pl.BlockSpec((d, C_END), const2),
            pl.BlockSpec((1, LANES), const2),
            pl.BlockSpec((1, LANES), const2),
            pl.BlockSpec((IN_TM, LANES), tab),
            pl.BlockSpec((IN_TM, LANES), tab),
        ],
        out_specs=[pl.BlockSpec((1, IN_TM, n), row) for n in widths],
        out_shape=[jax.ShapeDtypeStruct((b, s, n), t) for n, t in zip(widths, dtypes)],
        scratch_shapes=[pltpu.VMEM((d, C_END), jnp.bfloat16)],
        compiler_params=pltpu.CompilerParams(
            dimension_semantics=("arbitrary", "arbitrary"), vmem_limit_bytes=GQA_VMEM_LIMIT),
        name="in_proj",
    )(x, gain, w, qgain, kgain, cos, sin_signed)


def _pair_attention(q, k, v, bias, live_cols=None):
    lo = _lane_lo()
    tq = q.shape[0]
    one = jnp.ones((), v.dtype)
    zero = jnp.zeros((), q.dtype)
    qs = jnp.concatenate([jnp.where(lo, q, zero), jnp.where(lo, zero, q)], axis=0)
    vs = jnp.concatenate([jnp.where(lo, v, one), jnp.where(lo, one, v)], axis=1)
    s = lax.dot_general(qs, k, (((1,), (1,)), ((), ())), preferred_element_type=jnp.float32)
    if bias is None:
        m = jnp.max(s, axis=-1, keepdims=True)
        p = jnp.exp2(s - m).astype(jnp.bfloat16)
    else:
        tk = k.shape[0]
        slabs = []
        for n, (c0, c1) in enumerate(live_cols):
            r = slice(n * GRID_W, (n + 1) * GRID_W)
            sb = s[r, c0:c1] + bias[r, c0:c1]
            pb = jnp.exp2(sb - jnp.max(sb, axis=-1, keepdims=True)).astype(jnp.bfloat16)
            parts = [jnp.zeros((GRID_W, c0), pb.dtype)] * (c0 > 0) + [pb]
            parts += [jnp.zeros((GRID_W, tk - c1), pb.dtype)] * (c1 < tk)
            slabs.append(jnp.concatenate(parts, axis=1) if len(parts) > 1 else pb)
        p = jnp.concatenate(slabs, axis=0)
    o2 = jnp.dot(p, vs, preferred_element_type=jnp.float32)
    a = o2[:tq, :LANES]
    b = o2[tq:, LANES:]
    numer = jnp.where(lo, a, b)
    denom = jnp.where(lo, pltpu.roll(a, HEAD_DIM, 1), pltpu.roll(b, HEAD_DIM, 1))
    return numer / denom


def _gqa_out_kernel(qa_ref, kva_ref, g_ref, mb_ref, x_ref, w_ref, gain_ref, o_ref, ma_ref):
    pairs_per_kv = (A_HEADS // A_KV_HEADS) // 2
    subs_per_out = GQA_OUT_ROWS // GQA_SUB_TQ
    for sub in range(GQA_TQ // GQA_SUB_TQ):
        rows = slice(sub * GQA_SUB_TQ, (sub + 1) * GQA_SUB_TQ)
        for j in range(A_WIDTH // LANES):
            kv = j // pairs_per_kv
            k = kva_ref[0, :, kv * LANES:(kv + 1) * LANES]
            v = kva_ref[0, :, (A_KV_HEADS + kv) * LANES:(A_KV_HEADS + kv + 1) * LANES]
            sl = slice(j * LANES, (j + 1) * LANES)
            o = _pair_attention(qa_ref[0, rows, sl], k, v, None)
            ma_ref[rows, sl] = (o * _silu(g_ref[0, rows, sl])).astype(jnp.bfloat16)
        if (sub + 1) % subs_per_out == 0:
            rows = slice((sub + 1) * GQA_SUB_TQ - GQA_OUT_ROWS, (sub + 1) * GQA_SUB_TQ)
            y = x_ref[0, rows, :]
            y = y + jnp.dot(ma_ref[rows, :], w_ref[:A_WIDTH, :], preferred_element_type=jnp.float32)
            y = y + jnp.dot(mb_ref[0, rows, :], w_ref[A_WIDTH:, :], preferred_element_type=jnp.float32)
            ms = jnp.mean(y * y, axis=-1, keepdims=True)
            o_ref[0, rows, :] = (y * lax.rsqrt(ms + EPS)) * gain_ref[...]


def _gqa_out(qa, kva, ga, mb, x, w, gain):
    b, s, d = x.shape
    tile = lambda bi, qi: (bi, qi, 0)
    const2 = lambda bi, qi: (0, 0)
    return pl.pallas_call(
        _gqa_out_kernel,
        grid=(b, s // GQA_TQ),
        in_specs=[
            pl.BlockSpec((1, GQA_TQ, A_WIDTH), tile),
            pl.BlockSpec((1, s, kva.shape[2]), lambda bi, qi: (bi, 0, 0)),
            pl.BlockSpec((1, GQA_TQ, A_WIDTH), tile),
            pl.BlockSpec((1, GQA_TQ, B_WIDTH), tile),
            pl.BlockSpec((1, GQA_TQ, d), tile),
            pl.BlockSpec((A_WIDTH + B_WIDTH, d), const2),
            pl.BlockSpec((1, d), const2),
        ],
        out_specs=pl.BlockSpec((1, GQA_TQ, d), tile),
        out_shape=jax.ShapeDtypeStruct((b, s, d), jnp.float32),
        scratch_shapes=[pltpu.VMEM((GQA_TQ, A_WIDTH), jnp.bfloat16)],
        compiler_params=pltpu.CompilerParams(
            dimension_semantics=("arbitrary", "arbitrary"), vmem_limit_bytes=GQA_VMEM_LIMIT),
        name="gqa_out",
    )(qa, kva, ga, mb, x, w, gain)


def _na_first_key_row(first_query_row):
    return min(max(first_query_row - NA_KH // 2, 0), GRID_ROWS - NA_SUB_WIN)


def _na_build_bias(rpb_ref, bias_ref):
    shape = (GRID_W, LANES)
    qc = lax.broadcasted_iota(jnp.int32, shape, 0)
    lane = lax.broadcasted_iota(jnp.int32, shape, 1)
    kc = lane % GRID_W
    col_start = jnp.clip(qc - NA_KW // 2, 0, GRID_W - NA_KW)
    col_valid = (kc >= col_start) & (kc < col_start + NA_KW)
    valid = {(True, True): col_valid,
             (True, False): col_valid & (lane < GRID_W),
             (False, True): col_valid & (lane >= GRID_W)}
    for hh in range(rpb_ref.shape[0]):
        tiles = {}

        def pair_tile(dr):
            if dr not in tiles:
                row = jnp.broadcast_to(rpb_ref[hh, dr + NA_DR0:dr + NA_DR0 + 1, :], shape)
                tiles[dr] = pltpu.roll(row, LANES - (NA_KW - 1), 1, stride=1, stride_axis=0)
            return tiles[dr]

        for kind, first_row in enumerate((0, NA_SUB_ROWS, GRID_ROWS - NA_SUB_ROWS)):
            win0 = _na_first_key_row(first_row)
            for a in range(NA_SUB_ROWS):
                qr = first_row + a
                r0 = min(max(qr - NA_KH // 2, 0), GRID_ROWS - NA_KH)
                r = (hh % 2) * NA_SUB_TQ + a * GRID_W
                for kl in range(0, NA_SUB_WIN, 2):
                    row_valid = tuple(r0 <= win0 + kl + d < r0 + NA_KH for d in range(2))
                    if row_valid == (False, False):
                        tile = jnp.full(shape, MASK_VALUE, jnp.float32)
                    else:
                        tile = jnp.where(valid[row_valid], pair_tile(win0 + kl - qr), MASK_VALUE)
                    bias_ref[hh // 2, kind, r:r + GRID_W, kl * GRID_W:(kl + 2) * GRID_W] = tile


def _na_sub_block(sub):
    win0 = _na_first_key_row(sub * NA_SUB_ROWS)
    if sub == 0:
        return 0, win0, NA_KH, 0
    if sub == GRID_ROWS // NA_SUB_ROWS - 1:
        return 2, win0 + NA_SUB_WIN - NA_KH, NA_KH, NA_SUB_WIN - NA_KH
    return 1, win0, NA_SUB_WIN, 0


def _na_live_cols(sub, key0, n_keys):
    live = []
    for a in range(NA_SUB_ROWS):
        r0 = min(max(sub * NA_SUB_ROWS + a - NA_KH // 2, 0), GRID_ROWS - NA_KH)
        lo, hi = r0 - key0, r0 - key0 + NA_KH
        live.append(((lo // 2) * LANES, min(-(-hi // 2), n_keys // 2) * LANES))
    return live * 2


def _natten_kernel(q_ref, k_ref, v_ref, g_ref, rpb_ref, o_ref, bias_ref):
    @pl.when(pl.program_id(1) == 0)
    def _():
        _na_build_bias(rpb_ref, bias_ref)

    n_sub = GRID_ROWS // NA_SUB_ROWS
    for pair in range(NA_STEP_PAIRS):
        sl = slice(pair * LANES, (pair + 1) * LANES)
        for sub in range(n_sub):
            kind, key0, n_keys, col0 = _na_sub_block(sub)
            keys = slice(key0 * GRID_W, (key0 + n_keys) * GRID_W)
            cols = slice(col0 * GRID_W, (col0 + n_keys) * GRID_W)
            rows = slice(sub * NA_SUB_TQ, (sub + 1) * NA_SUB_TQ)
            o = _pair_attention(q_ref[0, rows, sl], k_ref[0, keys, sl], v_ref[0, keys, sl],
                                bias_ref[pair, kind, :, cols], _na_live_cols(sub, key0, n_keys))
            o_ref[0, rows, sl] = (o * _silu(g_ref[0, rows, sl])).astype(jnp.bfloat16)


def _natten(qkvb, gb, rpb_pairs):
    b, s, _ = qkvb.shape
    n_groups = B_WIDTH // (NA_STEP_PAIRS * LANES)
    width = NA_STEP_PAIRS * LANES
    col = lambda c: (lambda p, bi: (bi, 0, c * n_groups + p))
    return pl.pallas_call(
        _natten_kernel,
        grid=(n_groups, b),
        in_specs=[
            pl.BlockSpec((1, s, width), col(0)),
            pl.BlockSpec((1, s, width), col(1)),
            pl.BlockSpec((1, s, width), col(2)),
            pl.BlockSpec((1, s, width), col(0)),
            pl.BlockSpec((2 * NA_STEP_PAIRS, NA_DR_ROWS, LANES), lambda p, bi: (p, 0, 0)),
        ],
        out_specs=pl.BlockSpec((1, s, width), col(0)),
        out_shape=jax.ShapeDtypeStruct((b, s, B_WIDTH), jnp.bfloat16),
        scratch_shapes=[pltpu.VMEM((NA_STEP_PAIRS, 3, 2 * NA_SUB_TQ, NA_SUB_TK), jnp.float32)],
        compiler_params=pltpu.CompilerParams(
            dimension_semantics=("arbitrary", "arbitrary"), vmem_limit_bytes=VMEM_LIMIT),
        name="natten",
    )(qkvb, qkvb, qkvb, gb, rpb_pairs)


def _na_rpb_pairs(rpb):
    n_dr = 2 * NA_KH - 1
    half = jnp.pad(rpb * LOG2E, ((0, 0), (0, 0), (0, GRID_W - rpb.shape[2])))
    lo_pad = NA_DR0 - (NA_KH - 1)
    ext = jnp.pad(half, ((0, 0), (lo_pad, NA_DR_ROWS + 1 - lo_pad - n_dr), (0, 0)))
    return jnp.concatenate([ext[:, :-1], ext[:, 1:]], axis=-1)


def _rope_tables(seq):
    t = np.arange(seq)
    inv = ROPE_THETA ** (-np.arange(ROPE_PAIRS, dtype=np.float64) * (2.0 / (HEAD_DIM // 2)))
    ang_r = (t // GRID_W)[:, None] * inv[None, :]
    ang_c = (t % GRID_W)[:, None] * inv[None, :]
    ang = np.concatenate([ang_r, ang_r, ang_c, ang_c], axis=-1)
    sign = np.where(np.arange(HEAD_DIM) % (2 * ROPE_PAIRS) < ROPE_PAIRS, -1.0, 1.0)
    reps = (1, LANES // HEAD_DIM)
    cos = np.tile(np.cos(ang), reps).astype(np.float32)
    sin_signed = np.tile(np.sin(ang) * sign, reps).astype(np.float32)
    return jnp.asarray(cos), jnp.asarray(sin_signed)


def kernel(x, norm_gain, w_in, q_norm_a, k_norm_a, na_rpb, w_out, final_norm_gain):
    assert w_in.shape[0] == 1, "single-layer trunk only"
    assert x.shape[1:] == (SEQ, D_MODEL) and w_in.shape[2] == C_END
    cos, sin_signed = _rope_tables(SEQ)
    reps = LANES // HEAD_DIM
    qgain = jnp.tile(q_norm_a[0] * Q_SCALE, reps)[None, :]
    kgain = jnp.tile(k_norm_a[0], reps)[None, :]
    qa, kva, ga, gb, qkvb = _in_proj(x, norm_gain, w_in[0], qgain, kgain, cos, sin_signed)
    mixed_b = _natten(qkvb, gb, _na_rpb_pairs(na_rpb[0]))
    return _gqa_out(qa, kva, ga, mixed_b, x, w_out[0].astype(jnp.bfloat16), final_norm_gain[None, :])
```

```python
import math

import jax
import jax.numpy as jnp
import numpy as np
from jax import lax
from jax.experimental import pallas as pl
from jax.experimental.pallas import tpu as pltpu

D_MODEL = 1024
SEQ = 2048
GRID_W = 64
GRID_ROWS = SEQ // GRID_W
HEAD_DIM = 64
A_HEADS = 8
A_KV_HEADS = 2
B_HEADS = 8
A_WIDTH = A_HEADS * HEAD_DIM
A_KV_WIDTH = A_KV_HEADS * HEAD_DIM
B_WIDTH = B_HEADS * HEAD_DIM
ROPE_PAIRS = HEAD_DIM // 4
ROPE_THETA = 10000.0
NA_KH = 8
NA_KW = 16
EPS = 1e-6
LOG2E = math.log2(math.e)
Q_SCALE = HEAD_DIM ** -0.5 * LOG2E

LANES = 128
MASK_VALUE = -1e30

IN_TM = 1024
GQA_TQ = 1024
GQA_SUB_TQ = 256
GQA_OUT_ROWS = 1024
NA_SUB_ROWS = 4
NA_SUB_WIN = NA_SUB_ROWS + NA_KH
NA_SUB_TQ = NA_SUB_ROWS * GRID_W
NA_SUB_TK = NA_SUB_WIN * GRID_W
NA_DR0 = 11
NA_DR_ROWS = 24
NA_STEP_PAIRS = 2
VMEM_LIMIT = 48 * 1024 * 1024
GQA_VMEM_LIMIT = 56 * 1024 * 1024

C_QA = 0
C_KA = C_QA + A_WIDTH
C_GA = C_KA + 2 * A_KV_WIDTH
C_QB = C_GA + A_WIDTH
C_KB = C_QB + B_WIDTH
C_GB = C_KB + 2 * B_WIDTH
C_END = C_GB + B_WIDTH


def _lane_lo():
    return lax.broadcasted_iota(jnp.int32, (1, LANES), 1) < HEAD_DIM


def _silu(g):
    return g * (1.0 / (1.0 + jnp.exp(-g)))


def _head_norm_rope(y, gain, cos, sin_signed):
    lo = _lane_lo()
    y2 = y * y
    s_lo = jnp.sum(jnp.where(lo, y2, 0.0), axis=-1, keepdims=True)
    s_hi = jnp.sum(jnp.where(lo, 0.0, y2), axis=-1, keepdims=True)
    ms = jnp.where(lo, s_lo, s_hi) * (1.0 / HEAD_DIM)
    yn = (y * lax.rsqrt(ms + EPS)) * gain
    lane = lax.broadcasted_iota(jnp.int32, (1, LANES), 1)
    first_half = lane % (2 * ROPE_PAIRS) < ROPE_PAIRS
    nxt = pltpu.roll(yn, LANES - ROPE_PAIRS, 1)
    prv = pltpu.roll(yn, ROPE_PAIRS, 1)
    rot = jnp.where(first_half, nxt, prv)
    return yn * cos + rot * sin_signed


def _dup_heads(y):
    lo = _lane_lo()
    swapped = pltpu.roll(y, HEAD_DIM, 1)
    return jnp.where(lo, y, swapped), jnp.where(lo, swapped, y)


def _in_proj_kernel(x_ref, gain_ref, w_ref, qgain_ref, kgain_ref, cos_ref, sin_ref,
                    qa_ref, kva_ref, ga_ref, gb_ref, qkvb_ref):
    x = x_ref[0]
    ms = jnp.mean(x * x, axis=-1, keepdims=True)
    h = ((x * lax.rsqrt(ms + EPS)) * gain_ref[...]).astype(jnp.bfloat16)
    cos = cos_ref[...]
    sin = sin_ref[...]
    bf16 = jnp.bfloat16

    qkva = jnp.dot(h, w_ref[:, C_QA:C_GA], preferred_element_type=jnp.float32)
    for j in range(A_WIDTH // LANES):
        sl = slice(j * LANES, (j + 1) * LANES)
        qa_ref[0, :, sl] = _head_norm_rope(qkva[:, sl], qgain_ref[...], cos, sin).astype(bf16)
    ka = _head_norm_rope(qkva[:, C_KA:C_KA + LANES], kgain_ref[...], cos, sin)
    va = qkva[:, C_KA + LANES:C_GA]
    for n, slab in enumerate(_dup_heads(ka) + _dup_heads(va)):
        kva_ref[0, :, n * LANES:(n + 1) * LANES] = slab.astype(bf16)

    ga_ref[0] = jnp.dot(h, w_ref[:, C_GA:C_QB], preferred_element_type=jnp.float32)
    qb = jnp.dot(h, w_ref[:, C_QB:C_KB], preferred_element_type=jnp.float32)
    qkvb_ref[0, :, :B_WIDTH] = (qb * Q_SCALE).astype(bf16)
    qkvb_ref[0, :, B_WIDTH:] = jnp.dot(h, w_ref[:, C_KB:C_GB], preferred_element_type=jnp.float32).astype(bf16)
    gb_ref[0] = jnp.dot(h, w_ref[:, C_GB:C_END], preferred_element_type=jnp.float32)


def _in_proj(x, gain, w, qgain, kgain, cos, sin_signed):
    b, s, d = x.shape
    row = lambda bi, si: (bi, si, 0)
    const2 = lambda bi, si: (0, 0)
    tab = lambda bi, si: (si, 0)
    widths = (A_WIDTH, 4 * LANES, A_WIDTH, B_WIDTH, 3 * B_WIDTH)
    dtypes = (jnp.bfloat16, jnp.bfloat16, jnp.float32, jnp.float32, jnp.bfloat16)
    return pl.pallas_call(
        _in_proj_kernel,
        grid=(b, s // IN_TM),
        in_specs=[
            pl.BlockSpec((1, IN_TM, d), row),
            pl.BlockSpec((1, d), const2),
            pl.BlockSpec((d, C_END), const2),
            pl.BlockSpec((1, LANES), const2),
            pl.BlockSpec((1, LANES), const2),
            pl.BlockSpec((IN_TM, LANES), tab),
            pl.BlockSpec((IN_TM, LANES), tab),
        ],
        out_specs=[pl.BlockSpec((1, IN_TM, n), row) for n in widths],
        out_shape=[jax.ShapeDtypeStruct((b, s, n), t) for n, t in zip(widths, dtypes)],
        compiler_params=pltpu.CompilerParams(
            dimension_semantics=("arbitrary", "arbitrary"), vmem_limit_bytes=VMEM_LIMIT),
        name="in_proj",
    )(x, gain, w, qgain, kgain, cos, sin_signed)


def _value_operand(v):
    lo = _lane_lo()
    one = jnp.ones((), v.dtype)
    return jnp.concatenate([jnp.where(lo, v, one), jnp.where(lo, one, v)], axis=1)


def _pair_attention(q, k, vs, bias, live_cols=None):
    lo = _lane_lo()
    tq = q.shape[0]
    zero = jnp.zeros((), q.dtype)
    qs = jnp.concatenate([jnp.where(lo, q, zero), jnp.where(lo, zero, q)], axis=0)
    s = lax.dot_general(qs, k, (((1,), (1,)), ((), ())), preferred_element_type=jnp.float32)
    if bias is None:
        m = jnp.max(s, axis=-1, keepdims=True)
        p = jnp.exp2(s - m).astype(jnp.bfloat16)
    else:
        tk = k.shape[0]
        slabs = []
        for n, (c0, c1) in enumerate(live_cols):
            r = slice(n * GRID_W, (n + 1) * GRID_W)
            sb = s[r, c0:c1] + bias[r, c0:c1]
            pb = jnp.exp2(sb - jnp.max(sb, axis=-1, keepdims=True)).astype(jnp.bfloat16)
            parts = [jnp.zeros((GRID_W, c0), pb.dtype)] * (c0 > 0) + [pb]
            parts += [jnp.zeros((GRID_W, tk - c1), pb.dtype)] * (c1 < tk)
            slabs.append(jnp.concatenate(parts, axis=1) if len(parts) > 1 else pb)
        p = jnp.concatenate(slabs, axis=0)
    o2 = jnp.dot(p, vs, preferred_element_type=jnp.float32)
    a = o2[:tq, :LANES]
    b = o2[tq:, LANES:]
    numer = jnp.where(lo, a, b)
    denom = jnp.where(lo, pltpu.roll(a, HEAD_DIM, 1), pltpu.roll(b, HEAD_DIM, 1))
    return numer / denom


def _gqa_out_kernel(qa_ref, kva_ref, g_ref, mb_ref, x_ref, w_ref, gain_ref, o_ref, ma_ref):
    pairs_per_kv = (A_HEADS // A_KV_HEADS) // 2
    subs_per_out = GQA_OUT_ROWS // GQA_SUB_TQ
    vs = [_value_operand(kva_ref[0, :, (A_KV_HEADS + kv) * LANES:(A_KV_HEADS + kv + 1) * LANES])
          for kv in range(A_KV_HEADS)]
    for sub in range(GQA_TQ // GQA_SUB_TQ):
        rows = slice(sub * GQA_SUB_TQ, (sub + 1) * GQA_SUB_TQ)
        for j in range(A_WIDTH // LANES):
            kv = j // pairs_per_kv
            k = kva_ref[0, :, kv * LANES:(kv + 1) * LANES]
            sl = slice(j * LANES, (j + 1) * LANES)
            o = _pair_attention(qa_ref[0, rows, sl], k, vs[kv], None)
            ma_ref[rows, sl] = (o * _silu(g_ref[0, rows, sl])).astype(jnp.bfloat16)
        if (sub + 1) % subs_per_out == 0:
            rows = slice((sub + 1) * GQA_SUB_TQ - GQA_OUT_ROWS, (sub + 1) * GQA_SUB_TQ)
            y = x_ref[0, rows, :]
            y = y + jnp.dot(ma_ref[rows, :], w_ref[:A_WIDTH, :], preferred_element_type=jnp.float32)
            y = y + jnp.dot(mb_ref[0, rows, :], w_ref[A_WIDTH:, :], preferred_element_type=jnp.float32)
            ms = jnp.mean(y * y, axis=-1, keepdims=True)
            o_ref[0, rows, :] = (y * lax.rsqrt(ms + EPS)) * gain_ref[...]


def _gqa_out(qa, kva, ga, mb, x, w, gain):
    b, s, d = x.shape
    tile = lambda bi, qi: (bi, qi, 0)
    const2 = lambda bi, qi: (0, 0)
    return pl.pallas_call(
        _gqa_out_kernel,
        grid=(b, s // GQA_TQ),
        in_specs=[
            pl.BlockSpec((1, GQA_TQ, A_WIDTH), tile),
            pl.BlockSpec((1, s, kva.shape[2]), lambda bi, qi: (bi, 0, 0)),
            pl.BlockSpec((1, GQA_TQ, A_WIDTH), tile),
            pl.BlockSpec((1, GQA_TQ, B_WIDTH), tile),
            pl.BlockSpec((1, GQA_TQ, d), tile),
            pl.BlockSpec((A_WIDTH + B_WIDTH, d), const2),
            pl.BlockSpec((1, d), const2),
        ],
        out_specs=pl.BlockSpec((1, GQA_TQ, d), tile),
        out_shape=jax.ShapeDtypeStruct((b, s, d), jnp.float32),
        scratch_shapes=[pltpu.VMEM((GQA_TQ, A_WIDTH), jnp.bfloat16)],
        compiler_params=pltpu.CompilerParams(
            dimension_semantics=("arbitrary", "arbitrary"), vmem_limit_bytes=GQA_VMEM_LIMIT),
        name="gqa_out",
    )(qa, kva, ga, mb, x, w, gain)


def _na_first_key_row(first_query_row):
    return min(max(first_query_row - NA_KH // 2, 0), GRID_ROWS - NA_SUB_WIN)


def _na_build_bias(rpb_ref, bias_ref):
    shape = (GRID_W, LANES)
    qc = lax.broadcasted_iota(jnp.int32, shape, 0)
    lane = lax.broadcasted_iota(jnp.int32, shape, 1)
    kc = lane % GRID_W
    col_start = jnp.clip(qc - NA_KW // 2, 0, GRID_W - NA_KW)
    col_valid = (kc >= col_start) & (kc < col_start + NA_KW)
    valid = {(True, True): col_valid,
             (True, False): col_valid & (lane < GRID_W),
             (False, True): col_valid & (lane >= GRID_W)}
    for hh in range(rpb_ref.shape[0]):
        tiles = {}

        def pair_tile(dr):
            if dr not in tiles:
                row = jnp.broadcast_to(rpb_ref[hh, dr + NA_DR0:dr + NA_DR0 + 1, :], shape)
                tiles[dr] = pltpu.roll(row, LANES - (NA_KW - 1), 1, stride=1, stride_axis=0)
            return tiles[dr]

        for kind, first_row in enumerate((0, NA_SUB_ROWS, GRID_ROWS - NA_SUB_ROWS)):
            win0 = _na_first_key_row(first_row)
            for a in range(NA_SUB_ROWS):
                qr = first_row + a
                r0 = min(max(qr - NA_KH // 2, 0), GRID_ROWS - NA_KH)
                r = (hh % 2) * NA_SUB_TQ + a * GRID_W
                for kl in range(0, NA_SUB_WIN, 2):
                    row_valid = tuple(r0 <= win0 + kl + d < r0 + NA_KH for d in range(2))
                    if row_valid == (False, False):
                        tile = jnp.full(shape, MASK_VALUE, jnp.float32)
                    else:
                        tile = jnp.where(valid[row_valid], pair_tile(win0 + kl - qr), MASK_VALUE)
                    bias_ref[hh // 2, kind, r:r + GRID_W, kl * GRID_W:(kl + 2) * GRID_W] = tile


def _na_sub_block(sub):
    win0 = _na_first_key_row(sub * NA_SUB_ROWS)
    if sub == 0:
        return 0, win0, NA_KH, 0
    if sub == GRID_ROWS // NA_SUB_ROWS - 1:
        return 2, win0 + NA_SUB_WIN - NA_KH, NA_KH, NA_SUB_WIN - NA_KH
    return 1, win0, NA_SUB_WIN, 0


def _na_live_cols(sub, key0, n_keys):
    live = []
    for a in range(NA_SUB_ROWS):
        r0 = min(max(sub * NA_SUB_ROWS + a - NA_KH // 2, 0), GRID_ROWS - NA_KH)
        lo, hi = r0 - key0, r0 - key0 + NA_KH
        live.append(((lo // 2) * LANES, min(-(-hi // 2), n_keys // 2) * LANES))
    return live * 2


def _natten_kernel(q_ref, k_ref, v_ref, g_ref, rpb_ref, o_ref, bias_ref):
    @pl.when(pl.program_id(1) == 0)
    def _():
        _na_build_bias(rpb_ref, bias_ref)

    n_sub = GRID_ROWS // NA_SUB_ROWS
    for pair in range(NA_STEP_PAIRS):
        sl = slice(pair * LANES, (pair + 1) * LANES)
        vs = _value_operand(v_ref[0, :, sl])
        for sub in range(n_sub):
            kind, key0, n_keys, col0 = _na_sub_block(sub)
            keys = slice(key0 * GRID_W, (key0 + n_keys) * GRID_W)
            cols = slice(col0 * GRID_W, (col0 + n_keys) * GRID_W)
            rows = slice(sub * NA_SUB_TQ, (sub + 1) * NA_SUB_TQ)
            o = _pair_attention(q_ref[0, rows, sl], k_ref[0, keys, sl], vs[keys],
                                bias_ref[pair, kind, :, cols], _na_live_cols(sub, key0, n_keys))
            o_ref[0, rows, sl] = (o * _silu(g_ref[0, rows, sl])).astype(jnp.bfloat16)


def _natten(qkvb, gb, rpb_pairs):
    b, s, _ = qkvb.shape
    n_groups = B_WIDTH // (NA_STEP_PAIRS * LANES)
    width = NA_STEP_PAIRS * LANES
    col = lambda c: (lambda p, bi: (bi, 0, c * n_groups + p))
    return pl.pallas_call(
        _natten_kernel,
        grid=(n_groups, b),
        in_specs=[
            pl.BlockSpec((1, s, width), col(0)),
            pl.BlockSpec((1, s, width), col(1)),
            pl.BlockSpec((1, s, width), col(2)),
            pl.BlockSpec((1, s, width), col(0)),
            pl.BlockSpec((2 * NA_STEP_PAIRS, NA_DR_ROWS, LANES), lambda p, bi: (p, 0, 0)),
        ],
        out_specs=pl.BlockSpec((1, s, width), col(0)),
        out_shape=jax.ShapeDtypeStruct((b, s, B_WIDTH), jnp.bfloat16),
        scratch_shapes=[pltpu.VMEM((NA_STEP_PAIRS, 3, 2 * NA_SUB_TQ, NA_SUB_TK), jnp.float32)],
        compiler_params=pltpu.CompilerParams(
            dimension_semantics=("arbitrary", "arbitrary"), vmem_limit_bytes=VMEM_LIMIT),
        name="natten",
    )(qkvb, qkvb, qkvb, gb, rpb_pairs)


def _na_rpb_pairs(rpb):
    n_dr = 2 * NA_KH - 1
    half = jnp.pad(rpb * LOG2E, ((0, 0), (0, 0), (0, GRID_W - rpb.shape[2])))
    lo_pad = NA_DR0 - (NA_KH - 1)
    ext = jnp.pad(half, ((0, 0), (lo_pad, NA_DR_ROWS + 1 - lo_pad - n_dr), (0, 0)))
    return jnp.concatenate([ext[:, :-1], ext[:, 1:]], axis=-1)


def _rope_tables(seq):
    t = np.arange(seq)
    inv = ROPE_THETA ** (-np.arange(ROPE_PAIRS, dtype=np.float64) * (2.0 / (HEAD_DIM // 2)))
    ang_r = (t // GRID_W)[:, None] * inv[None, :]
    ang_c = (t % GRID_W)[:, None] * inv[None, :]
    ang = np.concatenate([ang_r, ang_r, ang_c, ang_c], axis=-1)
    sign = np.where(np.arange(HEAD_DIM) % (2 * ROPE_PAIRS) < ROPE_PAIRS, -1.0, 1.0)
    reps = (1, LANES // HEAD_DIM)
    cos = np.tile(np.cos(ang), reps).astype(np.float32)
    sin_signed = np.tile(np.sin(ang) * sign, reps).astype(np.float32)
    return jnp.asarray(cos), jnp.asarray(sin_signed)


def kernel(x, norm_gain, w_in, q_norm_a, k_norm_a, na_rpb, w_out, final_norm_gain):
    assert w_in.shape[0] == 1, "single-layer trunk only"
    assert x.shape[1:] == (SEQ, D_MODEL) and w_in.shape[2] == C_END
    cos, sin_signed = _rope_tables(SEQ)
    reps = LANES // HEAD_DIM
    qgain = jnp.tile(q_norm_a[0] * Q_SCALE, reps)[None, :]
    kgain = jnp.tile(k_norm_a[0], reps)[None, :]
    qa, kva, ga, gb, qkvb = _in_proj(x, norm_gain, w_in[0].astype(jnp.bfloat16), qgain, kgain, cos, sin_signed)
    mixed_b = _natten(qkvb, gb, _na_rpb_pairs(na_rpb[0]))
    return _gqa_out(qa, kva, ga, mixed_b, x, w_out[0].astype(jnp.bfloat16), final_norm_gain[None, :])
```

```python
import math

import jax
import jax.numpy as jnp
import numpy as np
from jax import lax
from jax.experimental import pallas as pl
from jax.experimental.pallas import tpu as pltpu

D_MODEL = 1024
SEQ = 2048
GRID_W = 64
GRID_ROWS = SEQ // GRID_W
HEAD_DIM = 64
A_HEADS = 8
A_KV_HEADS = 2
B_HEADS = 8
A_WIDTH = A_HEADS * HEAD_DIM
A_KV_WIDTH = A_KV_HEADS * HEAD_DIM
B_WIDTH = B_HEADS * HEAD_DIM
ROPE_PAIRS = HEAD_DIM // 4
ROPE_THETA = 10000.0
NA_KH = 8
NA_KW = 16
EPS = 1e-6
LOG2E = math.log2(math.e)
Q_SCALE = HEAD_DIM ** -0.5 * LOG2E

LANES = 128
MASK_VALUE = -1e30

IN_TM = 1024
GQA_TQ = 1024
GQA_SUB_TQ = 256
GQA_OUT_ROWS = 1024
NA_SUB_ROWS = 4
NA_SUB_WIN = NA_SUB_ROWS + NA_KH
NA_SUB_TQ = NA_SUB_ROWS * GRID_W
NA_SUB_TK = NA_SUB_WIN * GRID_W
NA_DR0 = 11
NA_DR_ROWS = 24
NA_STEP_PAIRS = 4
VMEM_LIMIT = 48 * 1024 * 1024
GQA_VMEM_LIMIT = 56 * 1024 * 1024

C_QA = 0
C_KA = C_QA + A_WIDTH
C_GA = C_KA + 2 * A_KV_WIDTH
C_QB = C_GA + A_WIDTH
C_KB = C_QB + B_WIDTH
C_GB = C_KB + 2 * B_WIDTH
C_END = C_GB + B_WIDTH


def _lane_lo():
    return lax.broadcasted_iota(jnp.int32, (1, LANES), 1) < HEAD_DIM


def _silu(g):
    return g * (1.0 / (1.0 + jnp.exp(-g)))


def _head_norm_rope(y, gain, cos, sin_signed):
    lo = _lane_lo()
    y2 = y * y
    s_lo = jnp.sum(jnp.where(lo, y2, 0.0), axis=-1, keepdims=True)
    s_hi = jnp.sum(jnp.where(lo, 0.0, y2), axis=-1, keepdims=True)
    ms = jnp.where(lo, s_lo, s_hi) * (1.0 / HEAD_DIM)
    yn = (y * lax.rsqrt(ms + EPS)) * gain
    lane = lax.broadcasted_iota(jnp.int32, (1, LANES), 1)
    first_half = lane % (2 * ROPE_PAIRS) < ROPE_PAIRS
    nxt = pltpu.roll(yn, LANES - ROPE_PAIRS, 1)
    prv = pltpu.roll(yn, ROPE_PAIRS, 1)
    rot = jnp.where(first_half, nxt, prv)
    return yn * cos + rot * sin_signed


def _dup_heads(y):
    lo = _lane_lo()
    swapped = pltpu.roll(y, HEAD_DIM, 1)
    return jnp.where(lo, y, swapped), jnp.where(lo, swapped, y)


def _in_proj_kernel(x_ref, gain_ref, w_ref, qgain_ref, kgain_ref, cos_ref, sin_ref,
                    qa_ref, kva_ref, ga_ref, gb_ref, qkvb_ref):
    x = x_ref[0]
    ms = jnp.mean(x * x, axis=-1, keepdims=True)
    h = ((x * lax.rsqrt(ms + EPS)) * gain_ref[...]).astype(jnp.bfloat16)
    cos = cos_ref[...]
    sin = sin_ref[...]
    bf16 = jnp.bfloat16

    qkva = jnp.dot(h, w_ref[:, C_QA:C_GA], preferred_element_type=jnp.float32)
    for j in range(A_WIDTH // LANES):
        sl = slice(j * LANES, (j + 1) * LANES)
        qa_ref[0, :, sl] = _head_norm_rope(qkva[:, sl], qgain_ref[...], cos, sin).astype(bf16)
    ka = _head_norm_rope(qkva[:, C_KA:C_KA + LANES], kgain_ref[...], cos, sin)
    va = qkva[:, C_KA + LANES:C_GA]
    for n, slab in enumerate(_dup_heads(ka) + _dup_heads(va)):
        kva_ref[0, :, n * LANES:(n + 1) * LANES] = slab.astype(bf16)

    ga_ref[0] = jnp.dot(h, w_ref[:, C_GA:C_QB], preferred_element_type=jnp.float32)
    qb = jnp.dot(h, w_ref[:, C_QB:C_KB], preferred_element_type=jnp.float32)
    qkvb_ref[0, :, :B_WIDTH] = (qb * Q_SCALE).astype(bf16)
    qkvb_ref[0, :, B_WIDTH:] = jnp.dot(h, w_ref[:, C_KB:C_GB], preferred_element_type=jnp.float32).astype(bf16)
    gb_ref[0] = jnp.dot(h, w_ref[:, C_GB:C_END], preferred_element_type=jnp.float32)


def _in_proj(x, gain, w, qgain, kgain, cos, sin_signed):
    b, s, d = x.shape
    row = lambda bi, si: (bi, si, 0)
    const2 = lambda bi, si: (0, 0)
    tab = lambda bi, si: (si, 0)
    widths = (A_WIDTH, 4 * LANES, A_WIDTH, B_WIDTH, 3 * B_WIDTH)
    dtypes = (jnp.bfloat16, jnp.bfloat16, jnp.float32, jnp.float32, jnp.bfloat16)
    return pl.pallas_call(
        _in_proj_kernel,
        grid=(b, s // IN_TM),
        in_specs=[
            pl.BlockSpec((1, IN_TM, d), row),
            pl.BlockSpec((1, d), const2),
            pl.BlockSpec((d, C_END), const2),
            pl.BlockSpec((1, LANES), const2),
            pl.BlockSpec((1, LANES), const2),
            pl.BlockSpec((IN_TM, LANES), tab),
            pl.BlockSpec((IN_TM, LANES), tab),
        ],
        out_specs=[pl.BlockSpec((1, IN_TM, n), row) for n in widths],
        out_shape=[jax.ShapeDtypeStruct((b, s, n), t) for n, t in zip(widths, dtypes)],
        compiler_params=pltpu.CompilerParams(
            dimension_semantics=("arbitrary", "arbitrary"), vmem_limit_bytes=VMEM_LIMIT),
        name="in_proj",
    )(x, gain, w, qgain, kgain, cos, sin_signed)


def _value_operand(v):
    lo = _lane_lo()
    one = jnp.ones((), v.dtype)
    return jnp.concatenate([jnp.where(lo, v, one), jnp.where(lo, one, v)], axis=1)


def _pair_attention(q, k, vs, bias, live_cols=None):
    lo = _lane_lo()
    tq = q.shape[0]
    zero = jnp.zeros((), q.dtype)
    qs = jnp.concatenate([jnp.where(lo, q, zero), jnp.where(lo, zero, q)], axis=0)
    s = lax.dot_general(qs, k, (((1,), (1,)), ((), ())), preferred_element_type=jnp.float32)
    if bias is None:
        m = jnp.max(s, axis=-1, keepdims=True)
        p = jnp.exp2(s - m).astype(jnp.bfloat16)
    else:
        tk = k.shape[0]
        slabs = []
        for n, (c0, c1) in enumerate(live_cols):
            r = slice(n * GRID_W, (n + 1) * GRID_W)
            sb = s[r, c0:c1] + bias[r, c0:c1]
            pb = jnp.exp2(sb - jnp.max(sb, axis=-1, keepdims=True)).astype(jnp.bfloat16)
            parts = [jnp.zeros((GRID_W, c0), pb.dtype)] * (c0 > 0) + [pb]
            parts += [jnp.zeros((GRID_W, tk - c1), pb.dtype)] * (c1 < tk)
            slabs.append(jnp.concatenate(parts, axis=1) if len(parts) > 1 else pb)
        p = jnp.concatenate(slabs, axis=0)
    o2 = jnp.dot(p, vs, preferred_element_type=jnp.float32)
    a = o2[:tq, :LANES]
    b = o2[tq:, LANES:]
    numer = jnp.where(lo, a, b)
    denom = jnp.where(lo, pltpu.roll(a, HEAD_DIM, 1), pltpu.roll(b, HEAD_DIM, 1))
    return numer / denom


def _gqa_out_kernel(qa_ref, kva_ref, g_ref, mb_ref, x_ref, w_ref, gain_ref, o_ref, ma_ref):
    pairs_per_kv = (A_HEADS // A_KV_HEADS) // 2
    subs_per_out = GQA_OUT_ROWS // GQA_SUB_TQ
    vs = [_value_operand(kva_ref[0, :, (A_KV_HEADS + kv) * LANES:(A_KV_HEADS + kv + 1) * LANES])
          for kv in range(A_KV_HEADS)]
    for sub in range(GQA_TQ // GQA_SUB_TQ):
        rows = slice(sub * GQA_SUB_TQ, (sub + 1) * GQA_SUB_TQ)
        for j in range(A_WIDTH // LANES):
            kv = j // pairs_per_kv
            k = kva_ref[0, :, kv * LANES:(kv + 1) * LANES]
            sl = slice(j * LANES, (j + 1) * LANES)
            o = _pair_attention(qa_ref[0, rows, sl], k, vs[kv], None)
            ma_ref[rows, sl] = (o * _silu(g_ref[0, rows, sl])).astype(jnp.bfloat16)
        if (sub + 1) % subs_per_out == 0:
            rows = slice((sub + 1) * GQA_SUB_TQ - GQA_OUT_ROWS, (sub + 1) * GQA_SUB_TQ)
            y = x_ref[0, rows, :]
            y = y + jnp.dot(ma_ref[rows, :], w_ref[:A_WIDTH, :], preferred_element_type=jnp.float32)
            y = y + jnp.dot(mb_ref[0, rows, :], w_ref[A_WIDTH:, :], preferred_element_type=jnp.float32)
            ms = jnp.mean(y * y, axis=-1, keepdims=True)
            o_ref[0, rows, :] = (y * lax.rsqrt(ms + EPS)) * gain_ref[...]


def _gqa_out(qa, kva, ga, mb, x, w, gain):
    b, s, d = x.shape
    tile = lambda bi, qi: (bi, qi, 0)
    const2 = lambda bi, qi: (0, 0)
    return pl.pallas_call(
        _gqa_out_kernel,
        grid=(b, s // GQA_TQ),
        in_specs=[
            pl.BlockSpec((1, GQA_TQ, A_WIDTH), tile),
            pl.BlockSpec((1, s, kva.shape[2]), lambda bi, qi: (bi, 0, 0)),
            pl.BlockSpec((1, GQA_TQ, A_WIDTH), tile),
            pl.BlockSpec((1, GQA_TQ, B_WIDTH), tile),
            pl.BlockSpec((1, GQA_TQ, d), tile),
            pl.BlockSpec((A_WIDTH + B_WIDTH, d), const2),
            pl.BlockSpec((1, d), const2),
        ],
        out_specs=pl.BlockSpec((1, GQA_TQ, d), tile),
        out_shape=jax.ShapeDtypeStruct((b, s, d), jnp.float32),
        scratch_shapes=[pltpu.VMEM((GQA_TQ, A_WIDTH), jnp.bfloat16)],
        compiler_params=pltpu.CompilerParams(
            dimension_semantics=("arbitrary", "arbitrary"), vmem_limit_bytes=GQA_VMEM_LIMIT),
        name="gqa_out",
    )(qa, kva, ga, mb, x, w, gain)


def _na_first_key_row(first_query_row):
    return min(max(first_query_row - NA_KH // 2, 0), GRID_ROWS - NA_SUB_WIN)


def _na_build_bias(rpb_ref, bias_ref):
    shape = (GRID_W, LANES)
    qc = lax.broadcasted_iota(jnp.int32, shape, 0)
    lane = lax.broadcasted_iota(jnp.int32, shape, 1)
    kc = lane % GRID_W
    col_start = jnp.clip(qc - NA_KW // 2, 0, GRID_W - NA_KW)
    col_valid = (kc >= col_start) & (kc < col_start + NA_KW)
    valid = {(True, True): col_valid,
             (True, False): col_valid & (lane < GRID_W),
             (False, True): col_valid & (lane >= GRID_W)}
    for hh in range(rpb_ref.shape[0]):
        tiles = {}

        def pair_tile(dr):
            if dr not in tiles:
                row = jnp.broadcast_to(rpb_ref[hh, dr + NA_DR0:dr + NA_DR0 + 1, :], shape)
                tiles[dr] = pltpu.roll(row, LANES - (NA_KW - 1), 1, stride=1, stride_axis=0)
            return tiles[dr]

        for kind, first_row in enumerate((0, NA_SUB_ROWS, GRID_ROWS - NA_SUB_ROWS)):
            win0 = _na_first_key_row(first_row)
            for a in range(NA_SUB_ROWS):
                qr = first_row + a
                r0 = min(max(qr - NA_KH // 2, 0), GRID_ROWS - NA_KH)
                r = (hh % 2) * NA_SUB_TQ + a * GRID_W
                for kl in range(0, NA_SUB_WIN, 2):
                    row_valid = tuple(r0 <= win0 + kl + d < r0 + NA_KH for d in range(2))
                    if row_valid == (False, False):
                        tile = jnp.full(shape, MASK_VALUE, jnp.float32)
                    else:
                        tile = jnp.where(valid[row_valid], pair_tile(win0 + kl - qr), MASK_VALUE)
                    bias_ref[hh // 2, kind, r:r + GRID_W, kl * GRID_W:(kl + 2) * GRID_W] = tile


def _na_sub_block(sub):
    win0 = _na_first_key_row(sub * NA_SUB_ROWS)
    if sub == 0:
        return 0, win0, NA_KH, 0
    if sub == GRID_ROWS // NA_SUB_ROWS - 1:
        return 2, win0 + NA_SUB_WIN - NA_KH, NA_KH, NA_SUB_WIN - NA_KH
    return 1, win0, NA_SUB_WIN, 0


def _na_live_cols(sub, key0, n_keys):
    live = []
    for a in range(NA_SUB_ROWS):
        r0 = min(max(sub * NA_SUB_ROWS + a - NA_KH // 2, 0), GRID_ROWS - NA_KH)
        lo, hi = r0 - key0, r0 - key0 + NA_KH
        live.append(((lo // 2) * LANES, min(-(-hi // 2), n_keys // 2) * LANES))
    return live * 2


def _natten_kernel(q_ref, k_ref, v_ref, g_ref, rpb_ref, o_ref, bias_ref):
    @pl.when(pl.program_id(1) == 0)
    def _():
        _na_build_bias(rpb_ref, bias_ref)

    n_sub = GRID_ROWS // NA_SUB_ROWS
    for pair in range(NA_STEP_PAIRS):
        sl = slice(pair * LANES, (pair + 1) * LANES)
        vs = _value_operand(v_ref[0, :, sl])
        for sub in range(n_sub):
            kind, key0, n_keys, col0 = _na_sub_block(sub)
            keys = slice(key0 * GRID_W, (key0 + n_keys) * GRID_W)
            cols = slice(col0 * GRID_W, (col0 + n_keys) * GRID_W)
            rows = slice(sub * NA_SUB_TQ, (sub + 1) * NA_SUB_TQ)
            o = _pair_attention(q_ref[0, rows, sl], k_ref[0, keys, sl], vs[keys],
                                bias_ref[pair, kind, :, cols], _na_live_cols(sub, key0, n_keys))
            o_ref[0, rows, sl] = (o * _silu(g_ref[0, rows, sl])).astype(jnp.bfloat16)


def _natten(qkvb, gb, rpb_pairs):
    b, s, _ = qkvb.shape
    n_groups = B_WIDTH // (NA_STEP_PAIRS * LANES)
    width = NA_STEP_PAIRS * LANES
    col = lambda c: (lambda p, bi: (bi, 0, c * n_groups + p))
    return pl.pallas_call(
        _natten_kernel,
        grid=(n_groups, b),
        in_specs=[
            pl.BlockSpec((1, s, width), col(0)),
            pl.BlockSpec((1, s, width), col(1)),
            pl.BlockSpec((1, s, width), col(2)),
            pl.BlockSpec((1, s, width), col(0)),
            pl.BlockSpec((2 * NA_STEP_PAIRS, NA_DR_ROWS, LANES), lambda p, bi: (p, 0, 0)),
        ],
        out_specs=pl.BlockSpec((1, s, width), col(0)),
        out_shape=jax.ShapeDtypeStruct((b, s, B_WIDTH), jnp.bfloat16),
        scratch_shapes=[pltpu.VMEM((NA_STEP_PAIRS, 3, 2 * NA_SUB_TQ, NA_SUB_TK), jnp.float32)],
        compiler_params=pltpu.CompilerParams(
            dimension_semantics=("arbitrary", "arbitrary"), vmem_limit_bytes=GQA_VMEM_LIMIT),
        name="natten",
    )(qkvb, qkvb, qkvb, gb, rpb_pairs)


def _na_rpb_pairs(rpb):
    n_dr = 2 * NA_KH - 1
    half = jnp.pad(rpb * LOG2E, ((0, 0), (0, 0), (0, GRID_W - rpb.shape[2])))
    lo_pad = NA_DR0 - (NA_KH - 1)
    ext = jnp.pad(half, ((0, 0), (lo_pad, NA_DR_ROWS + 1 - lo_pad - n_dr), (0, 0)))
    return jnp.concatenate([ext[:, :-1], ext[:, 1:]], axis=-1)


def _rope_tables(seq):
    t = np.arange(seq)
    inv = ROPE_THETA ** (-np.arange(ROPE_PAIRS, dtype=np.float64) * (2.0 / (HEAD_DIM // 2)))
    ang_r = (t // GRID_W)[:, None] * inv[None, :]
    ang_c = (t % GRID_W)[:, None] * inv[None, :]
    ang = np.concatenate([ang_r, ang_r, ang_c, ang_c], axis=-1)
    sign = np.where(np.arange(HEAD_DIM) % (2 * ROPE_PAIRS) < ROPE_PAIRS, -1.0, 1.0)
    reps = (1, LANES // HEAD_DIM)
    cos = np.tile(np.cos(ang), reps).astype(np.float32)
    sin_signed = np.tile(np.sin(ang) * sign, reps).astype(np.float32)
    return jnp.asarray(cos), jnp.asarray(sin_signed)


def kernel(x, norm_gain, w_in, q_norm_a, k_norm_a, na_rpb, w_out, final_norm_gain):
    assert w_in.shape[0] == 1, "single-layer trunk only"
    assert x.shape[1:] == (SEQ, D_MODEL) and w_in.shape[2] == C_END
    cos, sin_signed = _rope_tables(SEQ)
    reps = LANES // HEAD_DIM
    qgain = jnp.tile(q_norm_a[0] * Q_SCALE, reps)[None, :]
    kgain = jnp.tile(k_norm_a[0], reps)[None, :]
    qa, kva, ga, gb, qkvb = _in_proj(x, norm_gain, w_in[0].astype(jnp.bfloat16), qgain, kgain, cos, sin_signed)
    mixed_b = _natten(qkvb, gb, _na_rpb_pairs(na_rpb[0]))
    return _gqa_out(qa, kva, ga, mixed_b, x, w_out[0].astype(jnp.bfloat16), final_norm_gain[None, :])
```
